```python
import jax, jax.numpy as jnp
from jax import lax
import numpy as np

D_MODEL = 1024
BATCH = 4
SEQ = 4096
DEPTH = 1

MEM_LEN = 256
GRID_W = 64
EPS = 1e-6
HEAD_DIM = 64
ATTN_WIDTH = D_MODEL // 2
N_Q_HEADS = ATTN_WIDTH // HEAD_DIM
N_KV_HEADS = N_Q_HEADS // 4
Q_PER_KV = N_Q_HEADS // N_KV_HEADS
KV_WIDTH = N_KV_HEADS * HEAD_DIM
Q_BLOCK = 128
ROPE_THETA = 10000.0
ROPE_AXIS_DIM = HEAD_DIM // 2
ROPE_NFREQ = ROPE_AXIS_DIM // 2
GMLP_WIDTH = D_MODEL // 2
GMLP_GROUPS = 4
GMLP_GROUP_DIM = GMLP_WIDTH // GMLP_GROUPS
GMLP_CHUNK = 128
MEM_HEADS = 4
MEM_HEAD_DIM = 128
MEM_WIDTH = MEM_HEADS * MEM_HEAD_DIM
N_BRANCH = 3
IN_WIDTH = ATTN_WIDTH + 2 * KV_WIDTH + 2 * GMLP_WIDTH + MEM_WIDTH
D_FF = 2816

kernel_name = "hybrid_gated_gqa_gmlp_memory_macaron"


def rmsnorm(x, g):
    xf = x.astype(jnp.float32)
    y = xf * lax.rsqrt(jnp.mean(xf * xf, axis=-1, keepdims=True) + EPS)
    return (y * g.astype(jnp.float32)).astype(x.dtype)


def swiglu(h, w_gate, w_up, w_down):
    return (jax.nn.silu(h @ w_gate) * (h @ w_up)) @ w_down


def axial_rope_tables(seq, dtype):
    rows = seq // GRID_W
    row = jnp.repeat(jnp.arange(rows, dtype=jnp.float32), GRID_W)
    col = jnp.tile(jnp.arange(GRID_W, dtype=jnp.float32), rows)
    inv_freq = ROPE_THETA ** (-jnp.arange(ROPE_NFREQ, dtype=jnp.float32) / ROPE_NFREQ)
    ang = jnp.stack([row[:, None] * inv_freq, col[:, None] * inv_freq], axis=1)
    return jnp.cos(ang).astype(dtype), jnp.sin(ang).astype(dtype)


def apply_axial_rope(x, cos, sin):
    b, s, h, d = x.shape
    xr = x.reshape(b, s, h, 2, 2, ROPE_NFREQ)
    x1, x2 = xr[..., 0, :], xr[..., 1, :]
    c, sn = cos[None, :, None], sin[None, :, None]
    out = jnp.stack([x1 * c - x2 * sn, x2 * c + x1 * sn], axis=-2)
    return out.reshape(b, s, h, d)


def gqa_blocks(q, k, v):
    b, s, _, d = q.shape
    nq = s // Q_BLOCK
    scale = d ** -0.5
    qb = q.reshape(b, nq, Q_BLOCK, N_KV_HEADS, Q_PER_KV, d).transpose(1, 0, 2, 3, 4, 5)

    def attend(q_blk):
        sc = jnp.einsum('bqkgd,bskd->bkgqs', q_blk, k).astype(jnp.float32) * scale
        p = jax.nn.softmax(sc, axis=-1).astype(v.dtype)
        return jnp.einsum('bkgqs,bskd->bqkgd', p, v)

    o = lax.map(attend, qb)
    return o.transpose(1, 0, 2, 3, 4, 5).reshape(b, s, N_Q_HEADS * d)


def gmlp_spatial_gate(u, v, v_norm, w_s, b_s):
    b, s, _ = v.shape
    nc = s // GMLP_CHUNK
    vn = rmsnorm(v, v_norm).reshape(b, nc, GMLP_CHUNK, GMLP_GROUPS, GMLP_GROUP_DIM)
    mixed = jnp.einsum('gpq,bnqgc->bnpgc', w_s, vn) + b_s.T[None, None, :, :, None]
    return u * mixed.reshape(b, s, GMLP_WIDTH)


def memory_cross_attention(qm, km, vm):
    b, s, _, d = qm.shape
    sc = jnp.einsum('bshd,bmhd->bhsm', qm, km).astype(jnp.float32) * (d ** -0.5)
    p = jax.nn.softmax(sc, axis=-1).astype(vm.dtype)
    return jnp.einsum('bhsm,bmhd->bshd', p, vm).reshape(b, s, MEM_WIDTH)


def setup_inputs(seed: int = 0) -> dict:
    key = jax.random.key(seed)
    ks = iter(jax.random.split(key, 40))
    f32 = jnp.float32

    def w(shape, fan_in):
        return jax.random.normal(next(ks), shape, f32) * (fan_in ** -0.5)

    def gain(shape):
        return 1.0 + 0.05 * jax.random.normal(next(ks), shape, f32)

    def bias(shape):
        return 0.01 * jax.random.normal(next(ks), shape, f32)

    L = DEPTH
    return {
        "x": jax.random.normal(next(ks), (BATCH, SEQ, D_MODEL), f32),
        "mem": jax.random.normal(next(ks), (BATCH, MEM_LEN, D_MODEL), f32),
        "ffn1_pre": gain((L, D_MODEL)),
        "ffn1_w_gate": w((L, D_MODEL, D_FF), D_MODEL),
        "ffn1_w_up": w((L, D_MODEL, D_FF), D_MODEL),
        "ffn1_w_down": w((L, D_FF, D_MODEL), D_FF),
        "ffn1_post": gain((L, D_MODEL)),
        "mix_pre": gain((L, D_MODEL)),
        "mem_norm": gain((L, D_MODEL)),
        "w_in": w((L, D_MODEL, IN_WIDTH), D_MODEL),
        "w_mem_kv": w((L, D_MODEL, 2 * MEM_WIDTH), D_MODEL),
        "q_norm": gain((L, HEAD_DIM)),
        "k_norm": gain((L, HEAD_DIM)),
        "gmlp_v_norm": gain((L, GMLP_WIDTH)),
        "gmlp_w_s": w((L, GMLP_GROUPS, GMLP_CHUNK, GMLP_CHUNK), GMLP_CHUNK),
        "gmlp_b_s": 1.0 + bias((L, GMLP_GROUPS, GMLP_CHUNK)),
        "w_branch_gate": w((L, D_MODEL, N_BRANCH * D_MODEL), D_MODEL),
        "b_branch_gate": bias((L, N_BRANCH * D_MODEL)),
        "w_proj_attn": w((L, ATTN_WIDTH, D_MODEL), ATTN_WIDTH),
        "w_proj_gmlp": w((L, GMLP_WIDTH, D_MODEL), GMLP_WIDTH),
        "w_proj_mem": w((L, MEM_WIDTH, D_MODEL), MEM_WIDTH),
        "w_out": w((L, D_MODEL, D_MODEL), D_MODEL),
        "mix_post": gain((L, D_MODEL)),
        "ffn2_pre": gain((L, D_MODEL)),
        "ffn2_w_gate": w((L, D_MODEL, D_FF), D_MODEL),
        "ffn2_w_up": w((L, D_MODEL, D_FF), D_MODEL),
        "ffn2_w_down": w((L, D_FF, D_MODEL), D_FF),
        "ffn2_post": gain((L, D_MODEL)),
    }


def reference(x, mem, ffn1_pre, ffn1_w_gate, ffn1_w_up, ffn1_w_down, ffn1_post,
              mix_pre, mem_norm, w_in, w_mem_kv, q_norm, k_norm, gmlp_v_norm,
              gmlp_w_s, gmlp_b_s, w_branch_gate, b_branch_gate, w_proj_attn,
              w_proj_gmlp, w_proj_mem, w_out, mix_post, ffn2_pre, ffn2_w_gate,
              ffn2_w_up, ffn2_w_down, ffn2_post):
    b, s, _ = x.shape
    cos, sin = axial_rope_tables(s, x.dtype)
    splits = np.cumsum([ATTN_WIDTH, KV_WIDTH, KV_WIDTH, GMLP_WIDTH, GMLP_WIDTH]).tolist()

    for l in range(DEPTH):
        h = rmsnorm(x, ffn1_pre[l])
        x = x + 0.5 * rmsnorm(swiglu(h, ffn1_w_gate[l], ffn1_w_up[l], ffn1_w_down[l]), ffn1_post[l])

        h = rmsnorm(x, mix_pre[l])
        z = h @ w_in[l]
        q, k, v, gu, gv, qm = jnp.split(z, splits, axis=-1)

        q = apply_axial_rope(rmsnorm(q.reshape(b, s, N_Q_HEADS, HEAD_DIM), q_norm[l]), cos, sin)
        k = apply_axial_rope(rmsnorm(k.reshape(b, s, N_KV_HEADS, HEAD_DIM), k_norm[l]), cos, sin)
        v = v.reshape(b, s, N_KV_HEADS, HEAD_DIM)
        br_attn = gqa_blocks(q, k, v) @ w_proj_attn[l]

        gu = jax.nn.gelu(gu)
        gv = jax.nn.gelu(gv)
        br_gmlp = gmlp_spatial_gate(gu, gv, gmlp_v_norm[l], gmlp_w_s[l], gmlp_b_s[l]) @ w_proj_gmlp[l]

        kvm = rmsnorm(mem, mem_norm[l]) @ w_mem_kv[l]
        km, vm = jnp.split(kvm.reshape(b, MEM_LEN, 2, MEM_HEADS, MEM_HEAD_DIM), 2, axis=2)
        qm = qm.reshape(b, s, MEM_HEADS, MEM_HEAD_DIM)
        br_mem = memory_cross_attention(qm, km[:, :, 0], vm[:, :, 0]) @ w_proj_mem[l]

        gates = jax.nn.sigmoid(h @ w_branch_gate[l] + b_branch_gate[l]).reshape(b, s, N_BRANCH, D_MODEL)
        merged = gates[:, :, 0] * br_attn + gates[:, :, 1] * br_gmlp + gates[:, :, 2] * br_mem
        x = x + rmsnorm(merged @ w_out[l], mix_post[l])

        h = rmsnorm(x, ffn2_pre[l])
        x = x + 0.5 * rmsnorm(swiglu(h, ffn2_w_gate[l], ffn2_w_up[l], ffn2_w_down[l]), ffn2_post[l])
    return x
```

```python
import functools

import numpy as np
import jax
import jax.numpy as jnp
from jax import lax
from jax.experimental import pallas as pl
from jax.experimental.pallas import tpu as pltpu

D_MODEL = 1024
MEM_LEN = 256
GRID_W = 64
EPS = 1e-6
HEAD_DIM = 64
ATTN_WIDTH = 512
N_Q_HEADS = 8
N_KV_HEADS = 2
KV_WIDTH = 128
ROPE_THETA = 10000.0
ROPE_NFREQ = 16
GMLP_WIDTH = 512
GMLP_GROUPS = 4
GMLP_CHUNK = 128
MEM_HEADS = 4
MEM_HEAD_DIM = 128
MEM_WIDTH = 512
D_FF = 2816

LANES = 128
N_SLABS = ATTN_WIDTH // LANES
VMEM_LIMIT = 56 * 1024 * 1024

FFN_TM = 512
FFN_FC = 256
QKV_TM = 512
ATT_TQ = 256
MIX_TM = 256

BF16 = jnp.bfloat16
F32 = jnp.float32


def _rms(x, g):
    ms = jnp.mean(x * x, axis=-1, keepdims=True)
    return x * lax.rsqrt(ms + EPS) * g


def _dot(a, b):
    return jnp.dot(a, b, preferred_element_type=F32)


def _const_spec(shape):
    nd = len(shape)
    return pl.BlockSpec(shape, lambda *_: (0,) * nd, pipeline_mode=pl.Buffered(1))


def _ffn_kernel(x_ref, pre_ref, wg_ref, wu_ref, wd_ref, post_ref, o_ref):
    x = x_ref[...]
    h = _rms(x, pre_ref[...]).astype(BF16)
    acc = jnp.zeros((x.shape[0], D_MODEL), F32)
    for c in range(D_FF // FFN_FC):
        sl = slice(c * FFN_FC, (c + 1) * FFN_FC)
        g = _dot(h, wg_ref[:, sl])
        u = _dot(h, wu_ref[:, sl])
        a = (g * jax.nn.sigmoid(g) * u).astype(BF16)
        acc = acc + _dot(a, wd_ref[sl, :])
    o_ref[...] = x + 0.5 * _rms(acc, post_ref[...])


def _ffn(x2d, pre, wg, wu, wd, post):
    t = x2d.shape[0]
    return pl.pallas_call(
        _ffn_kernel,
        out_shape=jax.ShapeDtypeStruct((t, D_MODEL), F32),
        grid=(t // FFN_TM,),
        in_specs=[
            pl.BlockSpec((FFN_TM, D_MODEL), lambda i: (i, 0)),
            _const_spec((1, D_MODEL)),
            _const_spec((D_MODEL, D_FF)),
            _const_spec((D_MODEL, D_FF)),
            _const_spec((D_FF, D_MODEL)),
            _const_spec((1, D_MODEL)),
        ],
        out_specs=pl.BlockSpec((FFN_TM, D_MODEL), lambda i: (i, 0)),
        compiler_params=pltpu.CompilerParams(
            dimension_semantics=("arbitrary",), vmem_limit_bytes=VMEM_LIMIT),
        name="ffn",
    )(x2d, pre, wg, wu, wd, post)


def _head_norm(z, ones_bd, gain):
    z2 = z * z
    hi = z2.astype(BF16)
    lo = (z2 - hi.astype(F32)).astype(BF16)
    ssum = _dot(hi, ones_bd) + _dot(lo, ones_bd)
    return z * lax.rsqrt(ssum * (1.0 / HEAD_DIM) + EPS) * gain


def _rope(x, cos_t, sin_t, first_half):
    up = pltpu.roll(x, LANES - ROPE_NFREQ, axis=1)
    dn = pltpu.roll(x, ROPE_NFREQ, axis=1)
    return x * cos_t + jnp.where(first_half, up, dn) * sin_t


def _qkv_kernel(x_ref, pre_ref, w_ref, qg_ref, kg_ref, onesq_ref, onesk_ref,
                cos_ref, sin_ref, q_ref, kt_ref, v_ref):
    h = _rms(x_ref[...], pre_ref[...]).astype(BF16)
    z = _dot(h, w_ref[...])
    cos_t = cos_ref[...]
    sin_t = sin_ref[...]
    lane = lax.broadcasted_iota(jnp.int32, cos_t.shape, 1)
    first_half = (lane % (2 * ROPE_NFREQ)) < ROPE_NFREQ
    qn = _head_norm(z[:, :ATTN_WIDTH], onesq_ref[...], qg_ref[...])
    for j in range(N_SLABS):
        q_ref[j] = _rope(qn[:, j * LANES:(j + 1) * LANES], cos_t, sin_t, first_half).astype(BF16)
    kn = _head_norm(z[:, ATTN_WIDTH:ATTN_WIDTH + KV_WIDTH], onesk_ref[...], kg_ref[...])
    kr = _rope(kn, cos_t, sin_t, first_half)
    kt = kr.T.astype(BF16)
    kt_ref[0] = kt[:HEAD_DIM]
    kt_ref[1] = kt[HEAD_DIM:]
    v_ref[...] = z[:, ATTN_WIDTH + KV_WIDTH:].astype(BF16)


def _qkv(x3, pre, w_qkv, qg, kg, ones_q, ones_k, cos_t, sin_t):
    b, s, _ = x3.shape
    tm = QKV_TM
    return pl.pallas_call(
        _qkv_kernel,
        out_shape=(
            jax.ShapeDtypeStruct((b, N_SLABS, s, LANES), BF16),
            jax.ShapeDtypeStruct((b, N_KV_HEADS, HEAD_DIM, s), BF16),
            jax.ShapeDtypeStruct((b, s, KV_WIDTH), BF16),
        ),
        grid=(b, s // tm),
        in_specs=[
            pl.BlockSpec((None, tm, D_MODEL), lambda bi, i: (bi, i, 0)),
            _const_spec((1, D_MODEL)),
            _const_spec((D_MODEL, ATTN_WIDTH + 2 * KV_WIDTH)),
            _const_spec((1, ATTN_WIDTH)),
            _const_spec((1, KV_WIDTH)),
            _const_spec((ATTN_WIDTH, ATTN_WIDTH)),
            _const_spec((KV_WIDTH, KV_WIDTH)),
            pl.BlockSpec((tm, LANES), lambda bi, i: (i, 0)),
            pl.BlockSpec((tm, LANES), lambda bi, i: (i, 0)),
        ],
        out_specs=(
            pl.BlockSpec((None, N_SLABS, tm, LANES), lambda bi, i: (bi, 0, i, 0)),
            pl.BlockSpec((None, N_KV_HEADS, HEAD_DIM, tm), lambda bi, i: (bi, 0, 0, i)),
            pl.BlockSpec((None, tm, KV_WIDTH), lambda bi, i: (bi, i, 0)),
        ),
        compiler_params=pltpu.CompilerParams(
            dimension_semantics=("arbitrary", "arbitrary"), vmem_limit_bytes=VMEM_LIMIT),
        name="qkv",
    )(x3, pre, w_qkv, qg, kg, ones_q, ones_k, cos_t, sin_t)


def _memkv_kernel(m_ref, g_ref, w_ref, k_ref, v_ref):
    mn = _rms(m_ref[...], g_ref[...]).astype(BF16)
    kv = _dot(mn, w_ref[...])
    k_ref[...] = kv[:, :MEM_WIDTH].astype(BF16)
    v_ref[...] = kv[:, MEM_WIDTH:].astype(BF16)


def _memkv(mem, g, w):
    b = mem.shape[0]
    return pl.pallas_call(
        _memkv_kernel,
        out_shape=(
            jax.ShapeDtypeStruct((b, MEM_LEN, MEM_WIDTH), BF16),
            jax.ShapeDtypeStruct((b, MEM_LEN, MEM_WIDTH), BF16),
        ),
        grid=(b,),
        in_specs=[
            pl.BlockSpec((None, MEM_LEN, D_MODEL), lambda bi: (bi, 0, 0)),
            _const_spec((1, D_MODEL)),
            _const_spec((D_MODEL, 2 * MEM_WIDTH)),
        ],
        out_specs=(
            pl.BlockSpec((None, MEM_LEN, MEM_WIDTH), lambda bi: (bi, 0, 0)),
            pl.BlockSpec((None, MEM_LEN, MEM_WIDTH), lambda bi: (bi, 0, 0)),
        ),
        compiler_params=pltpu.CompilerParams(dimension_semantics=("arbitrary",)),
        name="memkv",
    )(mem, g, w)


def _attn_kernel(q_ref, kt_ref, v_ref, o_ref):
    q = q_ref[...]
    tq = q.shape[0]
    lane = lax.broadcasted_iota(jnp.int32, (tq, LANES), 1)
    lo_half = lane < HEAD_DIM
    zero = jnp.zeros_like(q)
    lhs = jnp.concatenate([jnp.where(lo_half, q, zero), jnp.where(lo_half, zero, q)], axis=0)
    kt = kt_ref[...]
    rhs = jnp.concatenate([kt, kt], axis=0)
    s = _dot(lhs, rhs)
    m = jnp.max(s, axis=-1, keepdims=True)
    p = jnp.exp(s - m)
    l = jnp.sum(p, axis=-1, keepdims=True)
    o = _dot(p.astype(BF16), v_ref[...]) / l
    grp = pl.program_id(1) // (N_SLABS // N_KV_HEADS)
    o_sw = pltpu.roll(o, HEAD_DIM, axis=1)
    even = jnp.where(grp == 0, o[:tq], o_sw[:tq])
    odd = jnp.where(grp == 0, o_sw[tq:], o[tq:])
    o_ref[...] = jnp.where(lo_half, even, odd).astype(BF16)


def _attn(q4, kt, v):
    b, _, s, _ = q4.shape
    tq = ATT_TQ
    per_grp = N_SLABS // N_KV_HEADS
    return pl.pallas_call(
        _attn_kernel,
        out_shape=jax.ShapeDtypeStruct((b, N_SLABS, s, LANES), BF16),
        grid=(b, N_SLABS, s // tq),
        in_specs=[
            pl.BlockSpec((None, None, tq, LANES), lambda bi, j, i: (bi, j, i, 0)),
            pl.BlockSpec((None, None, HEAD_DIM, s), lambda bi, j, i: (bi, j // per_grp, 0, 0)),
            pl.BlockSpec((None, s, KV_WIDTH), lambda bi, j, i: (bi, 0, 0)),
        ],
        out_specs=pl.BlockSpec((None, None, tq, LANES), lambda bi, j, i: (bi, j, i, 0)),
        compiler_params=pltpu.CompilerParams(
            dimension_semantics=("arbitrary", "arbitrary", "arbitrary"),
            vmem_limit_bytes=VMEM_LIMIT),
        name="attn",
    )(q4, kt, v)


def _mixer_kernel(x_ref, a_ref, pre_ref, wr_ref, wbg_ref, bbg_ref, gvn_ref, ws_ref, bs_ref,
                  km_ref, vm_ref, wpa_ref, wpg_ref, wpm_ref, wo_ref, post_ref, o_ref):
    x = x_ref[...]
    tm = x.shape[0]
    nch = tm // GMLP_CHUNK
    gd = GMLP_WIDTH // GMLP_GROUPS
    hb = _rms(x, pre_ref[...]).astype(BF16)
    z = _dot(hb, wr_ref[...])
    gu = jax.nn.gelu(z[:, :GMLP_WIDTH])
    gv = jax.nn.gelu(z[:, GMLP_WIDTH:2 * GMLP_WIDTH])
    qm = z[:, 2 * GMLP_WIDTH:].astype(BF16)

    vn = _rms(gv, gvn_ref[...]).astype(BF16)
    cols = []
    for g in range(GMLP_GROUPS):
        rhs = jnp.concatenate(
            [vn[c * GMLP_CHUNK:(c + 1) * GMLP_CHUNK, g * gd:(g + 1) * gd] for c in range(nch)], axis=1)
        mixed = _dot(ws_ref[g], rhs) + jnp.concatenate([bs_ref[g]] * nch, axis=1)
        cols.append(jnp.concatenate(
            [mixed[:, c * gd:(c + 1) * gd] for c in range(nch)], axis=0))
    gm = (gu * jnp.concatenate(cols, axis=1)).astype(BF16)

    outs = []
    for hh in range(MEM_HEADS):
        sl = slice(hh * MEM_HEAD_DIM, (hh + 1) * MEM_HEAD_DIM)
        s = lax.dot_general(qm[:, sl], km_ref[:, sl], (((1,), (1,)), ((), ())),
                            preferred_element_type=F32) * (MEM_HEAD_DIM ** -0.5)
        m = jnp.max(s, axis=-1, keepdims=True)
        p = jnp.exp(s - m)
        l = jnp.sum(p, axis=-1, keepdims=True)
        outs.append(_dot(p.astype(BF16), vm_ref[:, sl]) / l)
    mo = jnp.concatenate(outs, axis=1).astype(BF16)

    att = jnp.concatenate([a_ref[j] for j in range(N_SLABS)], axis=1)
    gates = jax.nn.sigmoid(_dot(hb, wbg_ref[...]) + bbg_ref[...])
    merged = (gates[:, :D_MODEL] * _dot(att, wpa_ref[...])
              + gates[:, D_MODEL:2 * D_MODEL] * _dot(gm, wpg_ref[...])
              + gates[:, 2 * D_MODEL:] * _dot(mo, wpm_ref[...]))
    out = _dot(merged.astype(BF16), wo_ref[...])
    o_ref[...] = x + _rms(out, post_ref[...])


def _mixer(x3, a4, pre, w_rest, w_bg, b_bg, gvn, ws, bs, km, vm, wpa, wpg, wpm, wo, post):
    b, s, _ = x3.shape
    tm = MIX_TM
    gd = GMLP_WIDTH // GMLP_GROUPS
    return pl.pallas_call(
        _mixer_kernel,
        out_shape=jax.ShapeDtypeStruct((b, s, D_MODEL), F32),
        grid=(b, s // tm),
        in_specs=[
            pl.BlockSpec((None, tm, D_MODEL), lambda bi, i: (bi, i, 0)),
            pl.BlockSpec((None, N_SLABS, tm, LANES), lambda bi, i: (bi, 0, i, 0)),
            _const_spec((1, D_MODEL)),
            _const_spec((D_MODEL, 2 * GMLP_WIDTH + MEM_WIDTH)),
            _const_spec((D_MODEL, 3 * D_MODEL)),
            _const_spec((1, 3 * D_MODEL)),
            _const_spec((1, GMLP_WIDTH)),
            _const_spec((GMLP_GROUPS, GMLP_CHUNK, GMLP_CHUNK)),
            _const_spec((GMLP_GROUPS, GMLP_CHUNK, gd)),
            pl.BlockSpec((None, MEM_LEN, MEM_WIDTH), lambda bi, i: (bi, 0, 0)),
            pl.BlockSpec((None, MEM_LEN, MEM_WIDTH), lambda bi, i: (bi, 0, 0)),
            _const_spec((ATTN_WIDTH, D_MODEL)),
            _const_spec((GMLP_WIDTH, D_MODEL)),
            _const_spec((MEM_WIDTH, D_MODEL)),
            _const_spec((D_MODEL, D_MODEL)),
            _const_spec((1, D_MODEL)),
        ],
        out_specs=pl.BlockSpec((None, tm, D_MODEL), lambda bi, i: (bi, i, 0)),
        compiler_params=pltpu.CompilerParams(
            dimension_semantics=("arbitrary", "arbitrary"), vmem_limit_bytes=VMEM_LIMIT),
        name="mixer",
    )(x3, a4, pre, w_rest, w_bg, b_bg, gvn, ws, bs, km, vm, wpa, wpg, wpm, wo, post)


def _rope_tables(seq):
    rows = seq // GRID_W
    row = jnp.repeat(jnp.arange(rows, dtype=F32), GRID_W)
    col = jnp.tile(jnp.arange(GRID_W, dtype=F32), rows)
    inv_freq = ROPE_THETA ** (-jnp.arange(ROPE_NFREQ, dtype=F32) / ROPE_NFREQ)
    ang = jnp.stack([row[:, None] * inv_freq, col[:, None] * inv_freq], axis=1)
    cos, sin = jnp.cos(ang), jnp.sin(ang)
    cos64 = jnp.concatenate([cos[:, 0], cos[:, 0], cos[:, 1], cos[:, 1]], axis=1)
    sin64 = jnp.concatenate([-sin[:, 0], sin[:, 0], -sin[:, 1], sin[:, 1]], axis=1)
    reps = LANES // HEAD_DIM
    return jnp.tile(cos64, (1, reps)), jnp.tile(sin64, (1, reps))


def _block_ones(width):
    idx = np.arange(width) // HEAD_DIM
    return jnp.asarray((idx[:, None] == idx[None, :]).astype(np.float32), dtype=BF16)


def kernel(x, mem, ffn1_pre, ffn1_w_gate, ffn1_w_up, ffn1_w_down, ffn1_post, mix_pre, mem_norm, w_in, w_mem_kv, q_norm, k_norm, gmlp_v_norm, gmlp_w_s, gmlp_b_s, w_branch_gate, b_branch_gate, w_proj_attn, w_proj_gmlp, w_proj_mem, w_out, mix_post, ffn2_pre, ffn2_w_gate, ffn2_w_up, ffn2_w_down, ffn2_post):
    b, s, d = x.shape
    depth = w_in.shape[0]
    cos_t, sin_t = _rope_tables(s)
    ones_q = _block_ones(ATTN_WIDTH)
    ones_k = _block_ones(KV_WIDTH)
    qkv_w = ATTN_WIDTH + 2 * KV_WIDTH
    gd = GMLP_WIDTH // GMLP_GROUPS
    row = lambda v: v.reshape(1, -1)
    bf = lambda w: w.astype(BF16)

    for l in range(depth):
        x = _ffn(x.reshape(b * s, d), row(ffn1_pre[l]), bf(ffn1_w_gate[l]), bf(ffn1_w_up[l]),
                 bf(ffn1_w_down[l]), row(ffn1_post[l])).reshape(b, s, d)

        qg = row(jnp.tile(q_norm[l], N_Q_HEADS)) * (HEAD_DIM ** -0.5)
        kg = row(jnp.tile(k_norm[l], N_KV_HEADS))
        q4, kt, v = _qkv(x, row(mix_pre[l]), bf(w_in[l][:, :qkv_w]), qg, kg, ones_q, ones_k,
                         cos_t, sin_t)
        km, vm = _memkv(mem, row(mem_norm[l]), bf(w_mem_kv[l]))
        a4 = _attn(q4, kt, v)
        bs = jnp.broadcast_to(gmlp_b_s[l][:, :, None], (GMLP_GROUPS, GMLP_CHUNK, gd))
        x = _mixer(x, a4, row(mix_pre[l]), bf(w_in[l][:, qkv_w:]), bf(w_branch_gate[l]),
                   row(b_branch_gate[l]), row(gmlp_v_norm[l]), bf(gmlp_w_s[l]), bs, km, vm,
                   bf(w_proj_attn[l]), bf(w_proj_gmlp[l]), bf(w_proj_mem[l]), bf(w_out[l]),
                   row(mix_post[l]))

        x = _ffn(x.reshape(b * s, d), row(ffn2_pre[l]), bf(ffn2_w_gate[l]), bf(ffn2_w_up[l]),
                 bf(ffn2_w_down[l]), row(ffn2_post[l])).reshape(b, s, d)
    return x
```

```python
import functools

import numpy as np
import jax
import jax.numpy as jnp
from jax import lax
from jax.experimental import pallas as pl
from jax.experimental.pallas import tpu as pltpu

D_MODEL = 1024
MEM_LEN = 256
GRID_W = 64
EPS = 1e-6
HEAD_DIM = 64
ATTN_WIDTH = 512
N_Q_HEADS = 8
N_KV_HEADS = 2
KV_WIDTH = 128
ROPE_THETA = 10000.0
ROPE_NFREQ = 16
GMLP_WIDTH = 512
GMLP_GROUPS = 4
GMLP_CHUNK = 128
MEM_HEADS = 4
MEM_HEAD_DIM = 128
MEM_WIDTH = 512
D_FF = 2816

LANES = 128
N_SLABS = ATTN_WIDTH // LANES
VMEM_LIMIT = 56 * 1024 * 1024

FFN_TM = 512
FFN_FC = 256
QKV_TM = 512
ATT_TQ = 256
MIX_TM = 256

BF16 = jnp.bfloat16
F32 = jnp.float32


def _rms(x, g):
    ms = jnp.mean(x * x, axis=-1, keepdims=True)
    return x * lax.rsqrt(ms + EPS) * g


def _dot(a, b):
    return jnp.dot(a, b, preferred_element_type=F32)


def _const_spec(shape):
    nd = len(shape)
    return pl.BlockSpec(shape, lambda *_: (0,) * nd, pipeline_mode=pl.Buffered(1))


def _ffn_kernel(x_ref, pre_ref, wg_ref, wu_ref, wd_ref, post_ref, o_ref):
    x = x_ref[...]
    h = _rms(x, pre_ref[...]).astype(BF16)
    acc = jnp.zeros((x.shape[0], D_MODEL), F32)
    for c in range(D_FF // FFN_FC):
        sl = slice(c * FFN_FC, (c + 1) * FFN_FC)
        g = _dot(h, wg_ref[:, sl])
        u = _dot(h, wu_ref[:, sl])
        a = (g * jax.nn.sigmoid(g) * u).astype(BF16)
        acc = acc + _dot(a, wd_ref[sl, :])
    o_ref[...] = x + 0.5 * _rms(acc, post_ref[...])


def _ffn(x2d, pre, wg, wu, wd, post):
    t = x2d.shape[0]
    return pl.pallas_call(
        _ffn_kernel,
        out_shape=jax.ShapeDtypeStruct((t, D_MODEL), F32),
        grid=(t // FFN_TM,),
        in_specs=[
            pl.BlockSpec((FFN_TM, D_MODEL), lambda i: (i, 0)),
            _const_spec((1, D_MODEL)),
            _const_spec((D_MODEL, D_FF)),
            _const_spec((D_MODEL, D_FF)),
            _const_spec((D_FF, D_MODEL)),
            _const_spec((1, D_MODEL)),
        ],
        out_specs=pl.BlockSpec((FFN_TM, D_MODEL), lambda i: (i, 0)),
        compiler_params=pltpu.CompilerParams(
            dimension_semantics=("arbitrary",), vmem_limit_bytes=VMEM_LIMIT),
        name="ffn",
    )(x2d, pre, wg, wu, wd, post)


def _head_norm(z, ones_bd, gain):
    z2 = z * z
    hi = z2.astype(BF16)
    lo = (z2 - hi.astype(F32)).astype(BF16)
    ssum = _dot(hi, ones_bd) + _dot(lo, ones_bd)
    return z * lax.rsqrt(ssum * (1.0 / HEAD_DIM) + EPS) * gain


def _rope(x, cos_t, sin_t, first_half):
    up = pltpu.roll(x, LANES - ROPE_NFREQ, axis=1)
    dn = pltpu.roll(x, ROPE_NFREQ, axis=1)
    return x * cos_t + jnp.where(first_half, up, dn) * sin_t


def _qkv_kernel(x_ref, pre_ref, w_ref, qg_ref, kg_ref, onesq_ref, onesk_ref,
                cos_ref, sin_ref, q_ref, kt_ref, v_ref):
    h = _rms(x_ref[...], pre_ref[...]).astype(BF16)
    z = _dot(h, w_ref[...])
    cos_t = cos_ref[...]
    sin_t = sin_ref[...]
    lane = lax.broadcasted_iota(jnp.int32, cos_t.shape, 1)
    first_half = (lane % (2 * ROPE_NFREQ)) < ROPE_NFREQ
    qn = _head_norm(z[:, :ATTN_WIDTH], onesq_ref[...], qg_ref[...])
    for j in range(N_SLABS):
        q_ref[j] = _rope(qn[:, j * LANES:(j + 1) * LANES], cos_t, sin_t, first_half).astype(BF16)
    kn = _head_norm(z[:, ATTN_WIDTH:ATTN_WIDTH + KV_WIDTH], onesk_ref[...], kg_ref[...])
    kr = _rope(kn, cos_t, sin_t, first_half)
    kt = kr.T.astype(BF16)
    kt_ref[0] = kt[:HEAD_DIM]
    kt_ref[1] = kt[HEAD_DIM:]
    v_ref[...] = z[:, ATTN_WIDTH + KV_WIDTH:].astype(BF16)


def _qkv(x3, pre, w_qkv, qg, kg, ones_q, ones_k, cos_t, sin_t):
    b, s, _ = x3.shape
    tm = QKV_TM
    return pl.pallas_call(
        _qkv_kernel,
        out_shape=(
            jax.ShapeDtypeStruct((b, N_SLABS, s, LANES), BF16),
            jax.ShapeDtypeStruct((b, N_KV_HEADS, HEAD_DIM, s), BF16),
            jax.ShapeDtypeStruct((b, s, KV_WIDTH), BF16),
        ),
        grid=(b, s // tm),
        in_specs=[
            pl.BlockSpec((None, tm, D_MODEL), lambda bi, i: (bi, i, 0)),
            _const_spec((1, D_MODEL)),
            _const_spec((D_MODEL, ATTN_WIDTH + 2 * KV_WIDTH)),
            _const_spec((1, ATTN_WIDTH)),
            _const_spec((1, KV_WIDTH)),
            _const_spec((ATTN_WIDTH, ATTN_WIDTH)),
            _const_spec((KV_WIDTH, KV_WIDTH)),
            pl.BlockSpec((tm, LANES), lambda bi, i: (i, 0)),
            pl.BlockSpec((tm, LANES), lambda bi, i: (i, 0)),
        ],
        out_specs=(
            pl.BlockSpec((None, N_SLABS, tm, LANES), lambda bi, i: (bi, 0, i, 0)),
            pl.BlockSpec((None, N_KV_HEADS, HEAD_DIM, tm), lambda bi, i: (bi, 0, 0, i)),
            pl.BlockSpec((None, tm, KV_WIDTH), lambda bi, i: (bi, i, 0)),
        ),
        compiler_params=pltpu.CompilerParams(
            dimension_semantics=("arbitrary", "arbitrary"), vmem_limit_bytes=VMEM_LIMIT),
        name="qkv",
    )(x3, pre, w_qkv, qg, kg, ones_q, ones_k, cos_t, sin_t)


def _memkv_kernel(m_ref, g_ref, w_ref, k_ref, v_ref):
    mn = _rms(m_ref[...], g_ref[...]).astype(BF16)
    kv = _dot(mn, w_ref[...])
    k_ref[...] = kv[:, :MEM_WIDTH].astype(BF16)
    v_ref[...] = kv[:, MEM_WIDTH:].astype(BF16)


def _memkv(mem, g, w):
    b = mem.shape[0]
    return pl.pallas_call(
        _memkv_kernel,
        out_shape=(
            jax.ShapeDtypeStruct((b, MEM_LEN, MEM_WIDTH), BF16),
            jax.ShapeDtypeStruct((b, MEM_LEN, MEM_WIDTH), BF16),
        ),
        grid=(b,),
        in_specs=[
            pl.BlockSpec((None, MEM_LEN, D_MODEL), lambda bi: (bi, 0, 0)),
            _const_spec((1, D_MODEL)),
            _const_spec((D_MODEL, 2 * MEM_WIDTH)),
        ],
        out_specs=(
            pl.BlockSpec((None, MEM_LEN, MEM_WIDTH), lambda bi: (bi, 0, 0)),
            pl.BlockSpec((None, MEM_LEN, MEM_WIDTH), lambda bi: (bi, 0, 0)),
        ),
        compiler_params=pltpu.CompilerParams(dimension_semantics=("arbitrary",)),
        name="memkv",
    )(mem, g, w)


def _attn_kernel(q_ref, kt_ref, v_ref, o_ref):
    tq = q_ref.shape[1]
    seq = v_ref.shape[0]
    lane = lax.broadcasted_iota(jnp.int32, (tq, LANES), 1)
    lo_half = lane < HEAD_DIM
    kt = kt_ref[...]
    rhs = jnp.concatenate([kt, kt], axis=0)
    v_ext = jnp.concatenate([v_ref[...], jnp.ones((seq, LANES), BF16)], axis=1)
    grp = pl.program_id(1)
    for j in range(N_SLABS // N_KV_HEADS):
        q = q_ref[j]
        zero = jnp.zeros_like(q)
        halves = []
        for par in range(2):
            lhs = jnp.where(lo_half, q, zero) if par == 0 else jnp.where(lo_half, zero, q)
            s = _dot(lhs, rhs)
            m = jnp.max(s, axis=-1, keepdims=True)
            p = jnp.exp2(s - m).astype(BF16)
            oe = _dot(p, v_ext)
            o = oe[:, :LANES] / oe[:, LANES:]
            o_sw = pltpu.roll(o, HEAD_DIM, axis=1)
            halves.append(jnp.where(grp == par, o, o_sw))
        o_ref[j] = jnp.where(lo_half, halves[0], halves[1]).astype(BF16)


def _attn(q4, kt, v):
    b, _, s, _ = q4.shape
    tq = ATT_TQ
    per_grp = N_SLABS // N_KV_HEADS
    return pl.pallas_call(
        _attn_kernel,
        out_shape=jax.ShapeDtypeStruct((b, N_SLABS, s, LANES), BF16),
        grid=(b, N_KV_HEADS, s // tq),
        in_specs=[
            pl.BlockSpec((None, per_grp, tq, LANES), lambda bi, g, i: (bi, g, i, 0)),
            pl.BlockSpec((None, None, HEAD_DIM, s), lambda bi, g, i: (bi, g, 0, 0)),
            pl.BlockSpec((None, s, KV_WIDTH), lambda bi, g, i: (bi, 0, 0)),
        ],
        out_specs=pl.BlockSpec((None, per_grp, tq, LANES), lambda bi, g, i: (bi, g, i, 0)),
        compiler_params=pltpu.CompilerParams(
            dimension_semantics=("arbitrary", "arbitrary", "arbitrary"),
            vmem_limit_bytes=VMEM_LIMIT),
        name="attn",
    )(q4, kt, v)


def _mixer_kernel(x_ref, a_ref, pre_ref, wr_ref, wbg_ref, bbg_ref, gvn_ref, ws_ref, bs_ref,
                  km_ref, vm_ref, wpa_ref, wpg_ref, wpm_ref, wo_ref, post_ref, o_ref):
    x = x_ref[...]
    tm = x.shape[0]
    nch = tm // GMLP_CHUNK
    gd = GMLP_WIDTH // GMLP_GROUPS
    hb = _rms(x, pre_ref[...]).astype(BF16)
    z = _dot(hb, wr_ref[...])
    gu = jax.nn.gelu(z[:, :GMLP_WIDTH])
    gv = jax.nn.gelu(z[:, GMLP_WIDTH:2 * GMLP_WIDTH])
    qm = z[:, 2 * GMLP_WIDTH:].astype(BF16)

    vn = _rms(gv, gvn_ref[...]).astype(BF16)
    cols = []
    for g in range(GMLP_GROUPS):
        rhs = jnp.concatenate(
            [vn[c * GMLP_CHUNK:(c + 1) * GMLP_CHUNK, g * gd:(g + 1) * gd] for c in range(nch)], axis=1)
        mixed = _dot(ws_ref[g], rhs) + jnp.concatenate([bs_ref[g]] * nch, axis=1)
        cols.append(jnp.concatenate(
            [mixed[:, c * gd:(c + 1) * gd] for c in range(nch)], axis=0))
    gm = (gu * jnp.concatenate(cols, axis=1)).astype(BF16)

    outs = []
    for hh in range(MEM_HEADS):
        sl = slice(hh * MEM_HEAD_DIM, (hh + 1) * MEM_HEAD_DIM)
        s = lax.dot_general(qm[:, sl], km_ref[:, sl], (((1,), (1,)), ((), ())),
                            preferred_element_type=F32) * (MEM_HEAD_DIM ** -0.5)
        m = jnp.max(s, axis=-1, keepdims=True)
        p = jnp.exp(s - m)
        l = jnp.sum(p, axis=-1, keepdims=True)
        outs.append(_dot(p.astype(BF16), vm_ref[:, sl]) / l)
    mo = jnp.concatenate(outs, axis=1).astype(BF16)

    att = jnp.concatenate([a_ref[j] for j in range(N_SLABS)], axis=1)
    gates = jax.nn.sigmoid(_dot(hb, wbg_ref[...]) + bbg_ref[...])
    merged = (gates[:, :D_MODEL] * _dot(att, wpa_ref[...])
              + gates[:, D_MODEL:2 * D_MODEL] * _dot(gm, wpg_ref[...])
              + gates[:, 2 * D_MODEL:] * _dot(mo, wpm_ref[...]))
    out = _dot(merged.astype(BF16), wo_ref[...])
    o_ref[...] = x + _rms(out, post_ref[...])


def _mixer(x3, a4, pre, w_rest, w_bg, b_bg, gvn, ws, bs, km, vm, wpa, wpg, wpm, wo, post):
    b, s, _ = x3.shape
    tm = MIX_TM
    gd = GMLP_WIDTH // GMLP_GROUPS
    return pl.pallas_call(
        _mixer_kernel,
        out_shape=jax.ShapeDtypeStruct((b, s, D_MODEL), F32),
        grid=(b, s // tm),
        in_specs=[
            pl.BlockSpec((None, tm, D_MODEL), lambda bi, i: (bi, i, 0)),
            pl.BlockSpec((None, N_SLABS, tm, LANES), lambda bi, i: (bi, 0, i, 0)),
            _const_spec((1, D_MODEL)),
            _const_spec((D_MODEL, 2 * GMLP_WIDTH + MEM_WIDTH)),
            _const_spec((D_MODEL, 3 * D_MODEL)),
            _const_spec((1, 3 * D_MODEL)),
            _const_spec((1, GMLP_WIDTH)),
            _const_spec((GMLP_GROUPS, GMLP_CHUNK, GMLP_CHUNK)),
            _const_spec((GMLP_GROUPS, GMLP_CHUNK, gd)),
            pl.BlockSpec((None, MEM_LEN, MEM_WIDTH), lambda bi, i: (bi, 0, 0)),
            pl.BlockSpec((None, MEM_LEN, MEM_WIDTH), lambda bi, i: (bi, 0, 0)),
            _const_spec((ATTN_WIDTH, D_MODEL)),
            _const_spec((GMLP_WIDTH, D_MODEL)),
            _const_spec((MEM_WIDTH, D_MODEL)),
            _const_spec((D_MODEL, D_MODEL)),
            _const_spec((1, D_MODEL)),
        ],
        out_specs=pl.BlockSpec((None, tm, D_MODEL), lambda bi, i: (bi, i, 0)),
        compiler_params=pltpu.CompilerParams(
            dimension_semantics=("arbitrary", "arbitrary"), vmem_limit_bytes=VMEM_LIMIT),
        name="mixer",
    )(x3, a4, pre, w_rest, w_bg, b_bg, gvn, ws, bs, km, vm, wpa, wpg, wpm, wo, post)


def _rope_tables(seq):
    rows = seq // GRID_W
    row = jnp.repeat(jnp.arange(rows, dtype=F32), GRID_W)
    col = jnp.tile(jnp.arange(GRID_W, dtype=F32), rows)
    inv_freq = ROPE_THETA ** (-jnp.arange(ROPE_NFREQ, dtype=F32) / ROPE_NFREQ)
    ang = jnp.stack([row[:, None] * inv_freq, col[:, None] * inv_freq], axis=1)
    cos, sin = jnp.cos(ang), jnp.sin(ang)
    cos64 = jnp.concatenate([cos[:, 0], cos[:, 0], cos[:, 1], cos[:, 1]], axis=1)
    sin64 = jnp.concatenate([-sin[:, 0], sin[:, 0], -sin[:, 1], sin[:, 1]], axis=1)
    reps = LANES // HEAD_DIM
    return jnp.tile(cos64, (1, reps)), jnp.tile(sin64, (1, reps))


def _block_ones(width):
    idx = np.arange(width) // HEAD_DIM
    return jnp.asarray((idx[:, None] == idx[None, :]).astype(np.float32), dtype=BF16)


def kernel(x, mem, ffn1_pre, ffn1_w_gate, ffn1_w_up, ffn1_w_down, ffn1_post, mix_pre, mem_norm, w_in, w_mem_kv, q_norm, k_norm, gmlp_v_norm, gmlp_w_s, gmlp_b_s, w_branch_gate, b_branch_gate, w_proj_attn, w_proj_gmlp, w_proj_mem, w_out, mix_post, ffn2_pre, ffn2_w_gate, ffn2_w_up, ffn2_w_down, ffn2_post):
    b, s, d = x.shape
    depth = w_in.shape[0]
    cos_t, sin_t = _rope_tables(s)
    ones_q = _block_ones(ATTN_WIDTH)
    ones_k = _block_ones(KV_WIDTH)
    qkv_w = ATTN_WIDTH + 2 * KV_WIDTH
    gd = GMLP_WIDTH // GMLP_GROUPS
    row = lambda v: v.reshape(1, -1)
    bf = lambda w: w.astype(BF16)

    for l in range(depth):
        x = _ffn(x.reshape(b * s, d), row(ffn1_pre[l]), bf(ffn1_w_gate[l]), bf(ffn1_w_up[l]),
                 bf(ffn1_w_down[l]), row(ffn1_post[l])).reshape(b, s, d)

        qg = row(jnp.tile(q_norm[l], N_Q_HEADS)) * (HEAD_DIM ** -0.5 * np.log2(np.e))
        kg = row(jnp.tile(k_norm[l], N_KV_HEADS))
        q4, kt, v = _qkv(x, row(mix_pre[l]), bf(w_in[l][:, :qkv_w]), qg, kg, ones_q, ones_k,
                         cos_t, sin_t)
        km, vm = _memkv(mem, row(mem_norm[l]), bf(w_mem_kv[l]))
        a4 = _attn(q4, kt, v)
        bs = jnp.broadcast_to(gmlp_b_s[l][:, :, None], (GMLP_GROUPS, GMLP_CHUNK, gd))
        x = _mixer(x, a4, row(mix_pre[l]), bf(w_in[l][:, qkv_w:]), bf(w_branch_gate[l]),
                   row(b_branch_gate[l]), row(gmlp_v_norm[l]), bf(gmlp_w_s[l]), bs, km, vm,
                   bf(w_proj_attn[l]), bf(w_proj_gmlp[l]), bf(w_proj_mem[l]), bf(w_out[l]),
                   row(mix_post[l]))

        x = _ffn(x.reshape(b * s, d), row(ffn2_pre[l]), bf(ffn2_w_gate[l]), bf(ffn2_w_up[l]),
                 bf(ffn2_w_down[l]), row(ffn2_post[l])).reshape(b, s, d)
    return x
```

```python
import functools

import numpy as np
import jax
import jax.numpy as jnp
from jax import lax
from jax.experimental import pallas as pl
from jax.experimental.pallas import tpu as pltpu

D_MODEL = 1024
MEM_LEN = 256
GRID_W = 64
EPS = 1e-6
HEAD_DIM = 64
ATTN_WIDTH = 512
N_Q_HEADS = 8
N_KV_HEADS = 2
KV_WIDTH = 128
ROPE_THETA = 10000.0
ROPE_NFREQ = 16
GMLP_WIDTH = 512
GMLP_GROUPS = 4
GMLP_CHUNK = 128
MEM_HEADS = 4
MEM_HEAD_DIM = 128
MEM_WIDTH = 512
D_FF = 2816

LANES = 128
N_SLABS = ATTN_WIDTH // LANES
VMEM_LIMIT = 56 * 1024 * 1024

FFN_TM = 512
FFN_FC = 256
QKV_TM = 512
ATT_TQ = 512
ATT_KC = 256
MIX_TM = 256

BF16 = jnp.bfloat16
F32 = jnp.float32


def _rms(x, g):
    ms = jnp.mean(x * x, axis=-1, keepdims=True)
    return x * lax.rsqrt(ms + EPS) * g


def _dot(a, b):
    return jnp.dot(a, b, preferred_element_type=F32)


def _const_spec(shape):
    nd = len(shape)
    return pl.BlockSpec(shape, lambda *_: (0,) * nd, pipeline_mode=pl.Buffered(1))


def _ffn_kernel(x_ref, pre_ref, wg_ref, wu_ref, wd_ref, post_ref, o_ref):
    x = x_ref[...]
    h = _rms(x, pre_ref[...]).astype(BF16)
    acc = jnp.zeros((x.shape[0], D_MODEL), F32)
    for c in range(D_FF // FFN_FC):
        sl = slice(c * FFN_FC, (c + 1) * FFN_FC)
        g = _dot(h, wg_ref[:, sl])
        u = _dot(h, wu_ref[:, sl])
        a = (g * jax.nn.sigmoid(g) * u).astype(BF16)
        acc = acc + _dot(a, wd_ref[sl, :])
    o_ref[...] = x + 0.5 * _rms(acc, post_ref[...])


def _ffn(x2d, pre, wg, wu, wd, post):
    t = x2d.shape[0]
    return pl.pallas_call(
        _ffn_kernel,
        out_shape=jax.ShapeDtypeStruct((t, D_MODEL), F32),
        grid=(t // FFN_TM,),
        in_specs=[
            pl.BlockSpec((FFN_TM, D_MODEL), lambda i: (i, 0)),
            _const_spec((1, D_MODEL)),
            _const_spec((D_MODEL, D_FF)),
            _const_spec((D_MODEL, D_FF)),
            _const_spec((D_FF, D_MODEL)),
            _const_spec((1, D_MODEL)),
        ],
        out_specs=pl.BlockSpec((FFN_TM, D_MODEL), lambda i: (i, 0)),
        compiler_params=pltpu.CompilerParams(
            dimension_semantics=("arbitrary",), vmem_limit_bytes=VMEM_LIMIT),
        name="ffn",
    )(x2d, pre, wg, wu, wd, post)


def _head_norm(z, ones_bd, gain):
    z2 = z * z
    hi = z2.astype(BF16)
    lo = (z2 - hi.astype(F32)).astype(BF16)
    ssum = _dot(hi, ones_bd) + _dot(lo, ones_bd)
    return z * lax.rsqrt(ssum * (1.0 / HEAD_DIM) + EPS) * gain


def _rope(x, cos_t, sin_t, first_half):
    up = pltpu.roll(x, LANES - ROPE_NFREQ, axis=1)
    dn = pltpu.roll(x, ROPE_NFREQ, axis=1)
    return x * cos_t + jnp.where(first_half, up, dn) * sin_t


def _qkv_kernel(x_ref, pre_ref, w_ref, qg_ref, kg_ref, onesq_ref, onesk_ref,
                cos_ref, sin_ref, q_ref, kt_ref, v_ref):
    h = _rms(x_ref[...], pre_ref[...]).astype(BF16)
    z = _dot(h, w_ref[...])
    cos_t = cos_ref[...]
    sin_t = sin_ref[...]
    lane = lax.broadcasted_iota(jnp.int32, cos_t.shape, 1)
    first_half = (lane % (2 * ROPE_NFREQ)) < ROPE_NFREQ
    qn = _head_norm(z[:, :ATTN_WIDTH], onesq_ref[...], qg_ref[...])
    for j in range(N_SLABS):
        q_ref[j] = _rope(qn[:, j * LANES:(j + 1) * LANES], cos_t, sin_t, first_half).astype(BF16)
    kn = _head_norm(z[:, ATTN_WIDTH:ATTN_WIDTH + KV_WIDTH], onesk_ref[...], kg_ref[...])
    kr = _rope(kn, cos_t, sin_t, first_half)
    kt = kr.T.astype(BF16)
    kt_ref[0] = kt[:HEAD_DIM]
    kt_ref[1] = kt[HEAD_DIM:]
    v_ref[...] = z[:, ATTN_WIDTH + KV_WIDTH:].astype(BF16)


def _qkv(x3, pre, w_qkv, qg, kg, ones_q, ones_k, cos_t, sin_t):
    b, s, _ = x3.shape
    tm = QKV_TM
    return pl.pallas_call(
        _qkv_kernel,
        out_shape=(
            jax.ShapeDtypeStruct((b, N_SLABS, s, LANES), BF16),
            jax.ShapeDtypeStruct((b, N_KV_HEADS, HEAD_DIM, s), BF16),
            jax.ShapeDtypeStruct((b, s, KV_WIDTH), BF16),
        ),
        grid=(b, s // tm),
        in_specs=[
            pl.BlockSpec((None, tm, D_MODEL), lambda bi, i: (bi, i, 0)),
            _const_spec((1, D_MODEL)),
            _const_spec((D_MODEL, ATTN_WIDTH + 2 * KV_WIDTH)),
            _const_spec((1, ATTN_WIDTH)),
            _const_spec((1, KV_WIDTH)),
            _const_spec((ATTN_WIDTH, ATTN_WIDTH)),
            _const_spec((KV_WIDTH, KV_WIDTH)),
            pl.BlockSpec((tm, LANES), lambda bi, i: (i, 0)),
            pl.BlockSpec((tm, LANES), lambda bi, i: (i, 0)),
        ],
        out_specs=(
            pl.BlockSpec((None, N_SLABS, tm, LANES), lambda bi, i: (bi, 0, i, 0)),
            pl.BlockSpec((None, N_KV_HEADS, HEAD_DIM, tm), lambda bi, i: (bi, 0, 0, i)),
            pl.BlockSpec((None, tm, KV_WIDTH), lambda bi, i: (bi, i, 0)),
        ),
        compiler_params=pltpu.CompilerParams(
            dimension_semantics=("arbitrary", "arbitrary"), vmem_limit_bytes=VMEM_LIMIT),
        name="qkv",
    )(x3, pre, w_qkv, qg, kg, ones_q, ones_k, cos_t, sin_t)


def _memkv_kernel(m_ref, g_ref, w_ref, k_ref, v_ref):
    mn = _rms(m_ref[...], g_ref[...]).astype(BF16)
    kv = _dot(mn, w_ref[...])
    k_ref[...] = kv[:, :MEM_WIDTH].astype(BF16)
    v_ref[...] = kv[:, MEM_WIDTH:].astype(BF16)


def _memkv(mem, g, w):
    b = mem.shape[0]
    return pl.pallas_call(
        _memkv_kernel,
        out_shape=(
            jax.ShapeDtypeStruct((b, MEM_LEN, MEM_WIDTH), BF16),
            jax.ShapeDtypeStruct((b, MEM_LEN, MEM_WIDTH), BF16),
        ),
        grid=(b,),
        in_specs=[
            pl.BlockSpec((None, MEM_LEN, D_MODEL), lambda bi: (bi, 0, 0)),
            _const_spec((1, D_MODEL)),
            _const_spec((D_MODEL, 2 * MEM_WIDTH)),
        ],
        out_specs=(
            pl.BlockSpec((None, MEM_LEN, MEM_WIDTH), lambda bi: (bi, 0, 0)),
            pl.BlockSpec((None, MEM_LEN, MEM_WIDTH), lambda bi: (bi, 0, 0)),
        ),
        compiler_params=pltpu.CompilerParams(dimension_semantics=("arbitrary",)),
        name="memkv",
    )(mem, g, w)


def _attn_kernel(q_ref, kt_ref, v_ref, o_ref):
    tq = q_ref.shape[1]
    seq = v_ref.shape[0]
    lane = lax.broadcasted_iota(jnp.int32, (tq, LANES), 1)
    lo_half = lane < HEAD_DIM
    ones = jnp.ones((ATT_KC, LANES), BF16)
    grp = pl.program_id(1)
    for j in range(q_ref.shape[0]):
        q = q_ref[j]
        zero = jnp.zeros_like(q)
        halves = []
        for par in range(2):
            lhs = jnp.where(lo_half, q, zero) if par == 0 else jnp.where(lo_half, zero, q)
            m = acc = None
            for c in range(seq // ATT_KC):
                ksl = slice(c * ATT_KC, (c + 1) * ATT_KC)
                kt = kt_ref[:, ksl]
                s = _dot(lhs, jnp.concatenate([kt, kt], axis=0))
                v_ext = jnp.concatenate([v_ref[ksl, :], ones], axis=1)
                mc = jnp.max(s, axis=-1, keepdims=True)
                if c == 0:
                    m = mc
                    acc = _dot(jnp.exp2(s - m).astype(BF16), v_ext)
                else:
                    m_new = jnp.maximum(m, mc)
                    acc = jnp.exp2(m - m_new) * acc + _dot(jnp.exp2(s - m_new).astype(BF16), v_ext)
                    m = m_new
            o = acc[:, :LANES] / acc[:, LANES:]
            o_sw = pltpu.roll(o, HEAD_DIM, axis=1)
            halves.append(jnp.where(grp == par, o, o_sw))
        o_ref[j] = jnp.where(lo_half, halves[0], halves[1]).astype(BF16)


def _attn(q4, kt, v):
    b, _, s, _ = q4.shape
    tq = ATT_TQ
    per_grp = N_SLABS // N_KV_HEADS
    return pl.pallas_call(
        _attn_kernel,
        out_shape=jax.ShapeDtypeStruct((b, N_SLABS, s, LANES), BF16),
        grid=(b, N_KV_HEADS, s // tq),
        in_specs=[
            pl.BlockSpec((None, per_grp, tq, LANES), lambda bi, g, i: (bi, g, i, 0)),
            pl.BlockSpec((None, None, HEAD_DIM, s), lambda bi, g, i: (bi, g, 0, 0)),
            pl.BlockSpec((None, s, KV_WIDTH), lambda bi, g, i: (bi, 0, 0)),
        ],
        out_specs=pl.BlockSpec((None, per_grp, tq, LANES), lambda bi, g, i: (bi, g, i, 0)),
        compiler_params=pltpu.CompilerParams(
            dimension_semantics=("arbitrary", "arbitrary", "arbitrary"),
            vmem_limit_bytes=VMEM_LIMIT),
        name="attn",
    )(q4, kt, v)


def _mixer_kernel(x_ref, a_ref, pre_ref, wr_ref, wbg_ref, bbg_ref, gvn_ref, ws_ref, bs_ref,
                  km_ref, vm_ref, wpa_ref, wpg_ref, wpm_ref, wo_ref, post_ref, o_ref):
    x = x_ref[...]
    tm = x.shape[0]
    nch = tm // GMLP_CHUNK
    gd = GMLP_WIDTH // GMLP_GROUPS
    hb = _rms(x, pre_ref[...]).astype(BF16)
    z = _dot(hb, wr_ref[...])
    gu = jax.nn.gelu(z[:, :GMLP_WIDTH])
    gv = jax.nn.gelu(z[:, GMLP_WIDTH:2 * GMLP_WIDTH])
    qm = z[:, 2 * GMLP_WIDTH:].astype(BF16)

    vn = _rms(gv, gvn_ref[...]).astype(BF16)
    cols = []
    for g in range(GMLP_GROUPS):
        rhs = jnp.concatenate(
            [vn[c * GMLP_CHUNK:(c + 1) * GMLP_CHUNK, g * gd:(g + 1) * gd] for c in range(nch)], axis=1)
        mixed = _dot(ws_ref[g], rhs) + jnp.concatenate([bs_ref[g]] * nch, axis=1)
        cols.append(jnp.concatenate(
            [mixed[:, c * gd:(c + 1) * gd] for c in range(nch)], axis=0))
    gm = (gu * jnp.concatenate(cols, axis=1)).astype(BF16)

    outs = []
    for hh in range(MEM_HEADS):
        sl = slice(hh * MEM_HEAD_DIM, (hh + 1) * MEM_HEAD_DIM)
        s = lax.dot_general(qm[:, sl], km_ref[:, sl], (((1,), (1,)), ((), ())),
                            preferred_element_type=F32) * (MEM_HEAD_DIM ** -0.5)
        m = jnp.max(s, axis=-1, keepdims=True)
        p = jnp.exp(s - m)
        l = jnp.sum(p, axis=-1, keepdims=True)
        outs.append(_dot(p.astype(BF16), vm_ref[:, sl]) / l)
    mo = jnp.concatenate(outs, axis=1).astype(BF16)

    att = jnp.concatenate([a_ref[j] for j in range(N_SLABS)], axis=1)
    gates = jax.nn.sigmoid(_dot(hb, wbg_ref[...]) + bbg_ref[...])
    merged = (gates[:, :D_MODEL] * _dot(att, wpa_ref[...])
              + gates[:, D_MODEL:2 * D_MODEL] * _dot(gm, wpg_ref[...])
              + gates[:, 2 * D_MODEL:] * _dot(mo, wpm_ref[...]))
    out = _dot(merged.astype(BF16), wo_ref[...])
    o_ref[...] = x + _rms(out, post_ref[...])


def _mixer(x3, a4, pre, w_rest, w_bg, b_bg, gvn, ws, bs, km, vm, wpa, wpg, wpm, wo, post):
    b, s, _ = x3.shape
    tm = MIX_TM
    gd = GMLP_WIDTH // GMLP_GROUPS
    return pl.pallas_call(
        _mixer_kernel,
        out_shape=jax.ShapeDtypeStruct((b, s, D_MODEL), F32),
        grid=(b, s // tm),
        in_specs=[
            pl.BlockSpec((None, tm, D_MODEL), lambda bi, i: (bi, i, 0)),
            pl.BlockSpec((None, N_SLABS, tm, LANES), lambda bi, i: (bi, 0, i, 0)),
            _const_spec((1, D_MODEL)),
            _const_spec((D_MODEL, 2 * GMLP_WIDTH + MEM_WIDTH)),
            _const_spec((D_MODEL, 3 * D_MODEL)),
            _const_spec((1, 3 * D_MODEL)),
            _const_spec((1, GMLP_WIDTH)),
            _const_spec((GMLP_GROUPS, GMLP_CHUNK, GMLP_CHUNK)),
            _const_spec((GMLP_GROUPS, GMLP_CHUNK, gd)),
            pl.BlockSpec((None, MEM_LEN, MEM_WIDTH), lambda bi, i: (bi, 0, 0)),
            pl.BlockSpec((None, MEM_LEN, MEM_WIDTH), lambda bi, i: (bi, 0, 0)),
            _const_spec((ATTN_WIDTH, D_MODEL)),
            _const_spec((GMLP_WIDTH, D_MODEL)),
            _const_spec((MEM_WIDTH, D_MODEL)),
            _const_spec((D_MODEL, D_MODEL)),
            _const_spec((1, D_MODEL)),
        ],
        out_specs=pl.BlockSpec((None, tm, D_MODEL), lambda bi, i: (bi, i, 0)),
        compiler_params=pltpu.CompilerParams(
            dimension_semantics=("arbitrary", "arbitrary"), vmem_limit_bytes=VMEM_LIMIT),
        name="mixer",
    )(x3, a4, pre, w_rest, w_bg, b_bg, gvn, ws, bs, km, vm, wpa, wpg, wpm, wo, post)


def _rope_tables(seq):
    rows = seq // GRID_W
    row = jnp.repeat(jnp.arange(rows, dtype=F32), GRID_W)
    col = jnp.tile(jnp.arange(GRID_W, dtype=F32), rows)
    inv_freq = ROPE_THETA ** (-jnp.arange(ROPE_NFREQ, dtype=F32) / ROPE_NFREQ)
    ang = jnp.stack([row[:, None] * inv_freq, col[:, None] * inv_freq], axis=1)
    cos, sin = jnp.cos(ang), jnp.sin(ang)
    cos64 = jnp.concatenate([cos[:, 0], cos[:, 0], cos[:, 1], cos[:, 1]], axis=1)
    sin64 = jnp.concatenate([-sin[:, 0], sin[:, 0], -sin[:, 1], sin[:, 1]], axis=1)
    reps = LANES // HEAD_DIM
    return jnp.tile(cos64, (1, reps)), jnp.tile(sin64, (1, reps))


def _block_ones(width):
    idx = np.arange(width) // HEAD_DIM
    return jnp.asarray((idx[:, None] == idx[None, :]).astype(np.float32), dtype=BF16)


def kernel(x, mem, ffn1_pre, ffn1_w_gate, ffn1_w_up, ffn1_w_down, ffn1_post, mix_pre, mem_norm, w_in, w_mem_kv, q_norm, k_norm, gmlp_v_norm, gmlp_w_s, gmlp_b_s, w_branch_gate, b_branch_gate, w_proj_attn, w_proj_gmlp, w_proj_mem, w_out, mix_post, ffn2_pre, ffn2_w_gate, ffn2_w_up, ffn2_w_down, ffn2_post):
    b, s, d = x.shape
    depth = w_in.shape[0]
    cos_t, sin_t = _rope_tables(s)
    ones_q = _block_ones(ATTN_WIDTH)
    ones_k = _block_ones(KV_WIDTH)
    qkv_w = ATTN_WIDTH + 2 * KV_WIDTH
    gd = GMLP_WIDTH // GMLP_GROUPS
    row = lambda v: v.reshape(1, -1)
    bf = lambda w: w.astype(BF16)

    for l in range(depth):
        x = _ffn(x.reshape(b * s, d), row(ffn1_pre[l]), bf(ffn1_w_gate[l]), bf(ffn1_w_up[l]),
                 bf(ffn1_w_down[l]), row(ffn1_post[l])).reshape(b, s, d)

        qg = row(jnp.tile(q_norm[l], N_Q_HEADS)) * (HEAD_DIM ** -0.5 * np.log2(np.e))
        kg = row(jnp.tile(k_norm[l], N_KV_HEADS))
        q4, kt, v = _qkv(x, row(mix_pre[l]), bf(w_in[l][:, :qkv_w]), qg, kg, ones_q, ones_k,
                         cos_t, sin_t)
        km, vm = _memkv(mem, row(mem_norm[l]), bf(w_mem_kv[l]))
        a4 = _attn(q4, kt, v)
        bs = jnp.broadcast_to(gmlp_b_s[l][:, :, None], (GMLP_GROUPS, GMLP_CHUNK, gd))
        x = _mixer(x, a4, row(mix_pre[l]), bf(w_in[l][:, qkv_w:]), bf(w_branch_gate[l]),
                   row(b_branch_gate[l]), row(gmlp_v_norm[l]), bf(gmlp_w_s[l]), bs, km, vm,
                   bf(w_proj_attn[l]), bf(w_proj_gmlp[l]), bf(w_proj_mem[l]), bf(w_out[l]),
                   row(mix_post[l]))

        x = _ffn(x.reshape(b * s, d), row(ffn2_pre[l]), bf(ffn2_w_gate[l]), bf(ffn2_w_up[l]),
                 bf(ffn2_w_down[l]), row(ffn2_post[l])).reshape(b, s, d)
    return x
```

```python
import functools

import numpy as np
import jax
import jax.numpy as jnp
from jax import lax
from jax.experimental import pallas as pl
from jax.experimental.pallas import tpu as pltpu

D_MODEL = 1024
MEM_LEN = 256
GRID_W = 64
EPS = 1e-6
HEAD_DIM = 64
ATTN_WIDTH = 512
N_Q_HEADS = 8
N_KV_HEADS = 2
KV_WIDTH = 128
ROPE_THETA = 10000.0
ROPE_NFREQ = 16
GMLP_WIDTH = 512
GMLP_GROUPS = 4
GMLP_CHUNK = 128
MEM_HEADS = 4
MEM_HEAD_DIM = 128
MEM_WIDTH = 512
D_FF = 2816

LANES = 128
N_SLABS = ATTN_WIDTH // LANES
VMEM_LIMIT = 56 * 1024 * 1024

FFN_TM = 1024
FFN_SUB = 512
FFN_FC = 256
QKV_TM = 512
ATT_TQ = 512
ATT_KC = 256
MIX_TM = 512

BF16 = jnp.bfloat16
F32 = jnp.float32


def _rms(x, g):
    ms = jnp.mean(x * x, axis=-1, keepdims=True)
    return x * lax.rsqrt(ms + EPS) * g


def _dot(a, b):
    return jnp.dot(a, b, preferred_element_type=F32)


def _const_spec(shape):
    nd = len(shape)
    return pl.BlockSpec(shape, lambda *_: (0,) * nd, pipeline_mode=pl.Buffered(1))


def _ffn_kernel(x_ref, pre_ref, wg_ref, wu_ref, wd_ref, post_ref, o_ref):
    n_sub = x_ref.shape[0] // FFN_SUB
    rows = [slice(r * FFN_SUB, (r + 1) * FFN_SUB) for r in range(n_sub)]
    hs = [_rms(x_ref[r, :], pre_ref[...]).astype(BF16) for r in rows]
    accs = [None] * n_sub
    for c in range(D_FF // FFN_FC):
        sl = slice(c * FFN_FC, (c + 1) * FFN_FC)
        for k in range(n_sub):
            g = _dot(hs[k], wg_ref[:, sl])
            u = _dot(hs[k], wu_ref[:, sl])
            a = (g * jax.nn.sigmoid(g) * u).astype(BF16)
            d = _dot(a, wd_ref[sl, :])
            accs[k] = d if c == 0 else accs[k] + d
    for k, r in enumerate(rows):
        o_ref[r, :] = x_ref[r, :] + 0.5 * _rms(accs[k], post_ref[...])


def _ffn(x2d, pre, wg, wu, wd, post):
    t = x2d.shape[0]
    return pl.pallas_call(
        _ffn_kernel,
        out_shape=jax.ShapeDtypeStruct((t, D_MODEL), F32),
        grid=(t // FFN_TM,),
        in_specs=[
            pl.BlockSpec((FFN_TM, D_MODEL), lambda i: (i, 0)),
            _const_spec((1, D_MODEL)),
            _const_spec((D_MODEL, D_FF)),
            _const_spec((D_MODEL, D_FF)),
            _const_spec((D_FF, D_MODEL)),
            _const_spec((1, D_MODEL)),
        ],
        out_specs=pl.BlockSpec((FFN_TM, D_MODEL), lambda i: (i, 0)),
        compiler_params=pltpu.CompilerParams(
            dimension_semantics=("arbitrary",), vmem_limit_bytes=VMEM_LIMIT),
        name="ffn",
    )(x2d, pre, wg, wu, wd, post)


def _head_norm(z, ones_bd, gain):
    z2 = z * z
    hi = z2.astype(BF16)
    lo = (z2 - hi.astype(F32)).astype(BF16)
    ssum = _dot(hi, ones_bd) + _dot(lo, ones_bd)
    return z * lax.rsqrt(ssum * (1.0 / HEAD_DIM) + EPS) * gain


def _rope(x, cos_t, sin_t, first_half):
    up = pltpu.roll(x, LANES - ROPE_NFREQ, axis=1)
    dn = pltpu.roll(x, ROPE_NFREQ, axis=1)
    return x * cos_t + jnp.where(first_half, up, dn) * sin_t


def _qkv_kernel(x_ref, pre_ref, w_ref, qg_ref, kg_ref, onesq_ref, onesk_ref,
                cos_ref, sin_ref, q_ref, kt_ref, v_ref):
    h = _rms(x_ref[...], pre_ref[...]).astype(BF16)
    z = _dot(h, w_ref[...])
    cos_t = cos_ref[...]
    sin_t = sin_ref[...]
    lane = lax.broadcasted_iota(jnp.int32, cos_t.shape, 1)
    first_half = (lane % (2 * ROPE_NFREQ)) < ROPE_NFREQ
    qn = _head_norm(z[:, :ATTN_WIDTH], onesq_ref[...], qg_ref[...])
    for j in range(N_SLABS):
        q_ref[j] = _rope(qn[:, j * LANES:(j + 1) * LANES], cos_t, sin_t, first_half).astype(BF16)
    kn = _head_norm(z[:, ATTN_WIDTH:ATTN_WIDTH + KV_WIDTH], onesk_ref[...], kg_ref[...])
    kr = _rope(kn, cos_t, sin_t, first_half)
    kt = kr.T.astype(BF16)
    kt_ref[0] = kt[:HEAD_DIM]
    kt_ref[1] = kt[HEAD_DIM:]
    v_ref[...] = z[:, ATTN_WIDTH + KV_WIDTH:].astype(BF16)


def _qkv(x3, pre, w_qkv, qg, kg, ones_q, ones_k, cos_t, sin_t):
    b, s, _ = x3.shape
    tm = QKV_TM
    return pl.pallas_call(
        _qkv_kernel,
        out_shape=(
            jax.ShapeDtypeStruct((b, N_SLABS, s, LANES), BF16),
            jax.ShapeDtypeStruct((b, N_KV_HEADS, HEAD_DIM, s), BF16),
            jax.ShapeDtypeStruct((b, s, KV_WIDTH), BF16),
        ),
        grid=(b, s // tm),
        in_specs=[
            pl.BlockSpec((None, tm, D_MODEL), lambda bi, i: (bi, i, 0)),
            _const_spec((1, D_MODEL)),
            _const_spec((D_MODEL, ATTN_WIDTH + 2 * KV_WIDTH)),
            _const_spec((1, ATTN_WIDTH)),
            _const_spec((1, KV_WIDTH)),
            _const_spec((ATTN_WIDTH, ATTN_WIDTH)),
            _const_spec((KV_WIDTH, KV_WIDTH)),
            pl.BlockSpec((tm, LANES), lambda bi, i: (i, 0)),
            pl.BlockSpec((tm, LANES), lambda bi, i: (i, 0)),
        ],
        out_specs=(
            pl.BlockSpec((None, N_SLABS, tm, LANES), lambda bi, i: (bi, 0, i, 0)),
            pl.BlockSpec((None, N_KV_HEADS, HEAD_DIM, tm), lambda bi, i: (bi, 0, 0, i)),
            pl.BlockSpec((None, tm, KV_WIDTH), lambda bi, i: (bi, i, 0)),
        ),
        compiler_params=pltpu.CompilerParams(
            dimension_semantics=("arbitrary", "arbitrary"), vmem_limit_bytes=VMEM_LIMIT),
        name="qkv",
    )(x3, pre, w_qkv, qg, kg, ones_q, ones_k, cos_t, sin_t)


def _memkv_kernel(m_ref, g_ref, w_ref, k_ref, v_ref):
    mn = _rms(m_ref[...], g_ref[...]).astype(BF16)
    kv = _dot(mn, w_ref[...])
    k_ref[...] = kv[:, :MEM_WIDTH].astype(BF16)
    v_ref[...] = kv[:, MEM_WIDTH:].astype(BF16)


def _memkv(mem, g, w):
    b = mem.shape[0]
    return pl.pallas_call(
        _memkv_kernel,
        out_shape=(
            jax.ShapeDtypeStruct((b, MEM_LEN, MEM_WIDTH), BF16),
            jax.ShapeDtypeStruct((b, MEM_LEN, MEM_WIDTH), BF16),
        ),
        grid=(b,),
        in_specs=[
            pl.BlockSpec((None, MEM_LEN, D_MODEL), lambda bi: (bi, 0, 0)),
            _const_spec((1, D_MODEL)),
            _const_spec((D_MODEL, 2 * MEM_WIDTH)),
        ],
        out_specs=(
            pl.BlockSpec((None, MEM_LEN, MEM_WIDTH), lambda bi: (bi, 0, 0)),
            pl.BlockSpec((None, MEM_LEN, MEM_WIDTH), lambda bi: (bi, 0, 0)),
        ),
        compiler_params=pltpu.CompilerParams(dimension_semantics=("arbitrary",)),
        name="memkv",
    )(mem, g, w)


def _attn_kernel(q_ref, kt_ref, v_ref, o_ref):
    tq = q_ref.shape[1]
    seq = v_ref.shape[0]
    lane = lax.broadcasted_iota(jnp.int32, (tq, LANES), 1)
    lo_half = lane < HEAD_DIM
    ones = jnp.ones((ATT_KC, LANES), BF16)
    grp = pl.program_id(1)
    for j in range(q_ref.shape[0]):
        q = q_ref[j]
        zero = jnp.zeros_like(q)
        halves = []
        for par in range(2):
            lhs = jnp.where(lo_half, q, zero) if par == 0 else jnp.where(lo_half, zero, q)
            m = acc = None
            for c in range(seq // ATT_KC):
                ksl = slice(c * ATT_KC, (c + 1) * ATT_KC)
                kt = kt_ref[:, ksl]
                s = _dot(lhs, jnp.concatenate([kt, kt], axis=0))
                v_ext = jnp.concatenate([v_ref[ksl, :], ones], axis=1)
                mc = jnp.max(s, axis=-1, keepdims=True)
                if c == 0:
                    m = mc
                    acc = _dot(jnp.exp2(s - m).astype(BF16), v_ext)
                else:
                    m_new = jnp.maximum(m, mc)
                    acc = jnp.exp2(m - m_new) * acc + _dot(jnp.exp2(s - m_new).astype(BF16), v_ext)
                    m = m_new
            o = acc[:, :LANES] / acc[:, LANES:]
            o_sw = pltpu.roll(o, HEAD_DIM, axis=1)
            halves.append(jnp.where(grp == par, o, o_sw))
        o_ref[j] = jnp.where(lo_half, halves[0], halves[1]).astype(BF16)


def _attn(q4, kt, v):
    b, _, s, _ = q4.shape
    tq = ATT_TQ
    per_grp = N_SLABS // N_KV_HEADS
    return pl.pallas_call(
        _attn_kernel,
        out_shape=jax.ShapeDtypeStruct((b, N_SLABS, s, LANES), BF16),
        grid=(b, N_KV_HEADS, s // tq),
        in_specs=[
            pl.BlockSpec((None, per_grp, tq, LANES), lambda bi, g, i: (bi, g, i, 0)),
            pl.BlockSpec((None, None, HEAD_DIM, s), lambda bi, g, i: (bi, g, 0, 0)),
            pl.BlockSpec((None, s, KV_WIDTH), lambda bi, g, i: (bi, 0, 0)),
        ],
        out_specs=pl.BlockSpec((None, per_grp, tq, LANES), lambda bi, g, i: (bi, g, i, 0)),
        compiler_params=pltpu.CompilerParams(
            dimension_semantics=("arbitrary", "arbitrary", "arbitrary"),
            vmem_limit_bytes=VMEM_LIMIT),
        name="attn",
    )(q4, kt, v)


def _mixer_kernel(x_ref, a_ref, pre_ref, wr_ref, wbg_ref, bbg_ref, gvn_ref, ws_ref, bs_ref,
                  km_ref, vm_ref, wpa_ref, wpg_ref, wpm_ref, wo_ref, post_ref, o_ref):
    x = x_ref[...]
    tm = x.shape[0]
    nch = tm // GMLP_CHUNK
    gd = GMLP_WIDTH // GMLP_GROUPS
    hb = _rms(x, pre_ref[...]).astype(BF16)
    z = _dot(hb, wr_ref[...])
    gu = jax.nn.gelu(z[:, :GMLP_WIDTH])
    gv = jax.nn.gelu(z[:, GMLP_WIDTH:2 * GMLP_WIDTH])
    qm = z[:, 2 * GMLP_WIDTH:].astype(BF16)

    vn = _rms(gv, gvn_ref[...]).astype(BF16)
    cols = []
    for g in range(GMLP_GROUPS):
        rhs = jnp.concatenate(
            [vn[c * GMLP_CHUNK:(c + 1) * GMLP_CHUNK, g * gd:(g + 1) * gd] for c in range(nch)], axis=1)
        mixed = _dot(ws_ref[g], rhs) + jnp.concatenate([bs_ref[g]] * nch, axis=1)
        cols.append(jnp.concatenate(
            [mixed[:, c * gd:(c + 1) * gd] for c in range(nch)], axis=0))
    gm = (gu * jnp.concatenate(cols, axis=1)).astype(BF16)

    outs = []
    for hh in range(MEM_HEADS):
        sl = slice(hh * MEM_HEAD_DIM, (hh + 1) * MEM_HEAD_DIM)
        s = lax.dot_general(qm[:, sl], km_ref[:, sl], (((1,), (1,)), ((), ())),
                            preferred_element_type=F32) * (MEM_HEAD_DIM ** -0.5)
        m = jnp.max(s, axis=-1, keepdims=True)
        p = jnp.exp(s - m)
        l = jnp.sum(p, axis=-1, keepdims=True)
        outs.append(_dot(p.astype(BF16), vm_ref[:, sl]) / l)
    mo = jnp.concatenate(outs, axis=1).astype(BF16)

    att = jnp.concatenate([a_ref[j] for j in range(N_SLABS)], axis=1)
    gates = jax.nn.sigmoid(_dot(hb, wbg_ref[...]) + bbg_ref[...])
    merged = (gates[:, :D_MODEL] * _dot(att, wpa_ref[...])
              + gates[:, D_MODEL:2 * D_MODEL] * _dot(gm, wpg_ref[...])
              + gates[:, 2 * D_MODEL:] * _dot(mo, wpm_ref[...]))
    out = _dot(merged.astype(BF16), wo_ref[...])
    o_ref[...] = x + _rms(out, post_ref[...])


def _mixer(x3, a4, pre, w_rest, w_bg, b_bg, gvn, ws, bs, km, vm, wpa, wpg, wpm, wo, post):
    b, s, _ = x3.shape
    tm = MIX_TM
    gd = GMLP_WIDTH // GMLP_GROUPS
    return pl.pallas_call(
        _mixer_kernel,
        out_shape=jax.ShapeDtypeStruct((b, s, D_MODEL), F32),
        grid=(b, s // tm),
        in_specs=[
            pl.BlockSpec((None, tm, D_MODEL), lambda bi, i: (bi, i, 0)),
            pl.BlockSpec((None, N_SLABS, tm, LANES), lambda bi, i: (bi, 0, i, 0)),
            _const_spec((1, D_MODEL)),
            _const_spec((D_MODEL, 2 * GMLP_WIDTH + MEM_WIDTH)),
            _const_spec((D_MODEL, 3 * D_MODEL)),
            _const_spec((1, 3 * D_MODEL)),
            _const_spec((1, GMLP_WIDTH)),
            _const_spec((GMLP_GROUPS, GMLP_CHUNK, GMLP_CHUNK)),
            _const_spec((GMLP_GROUPS, GMLP_CHUNK, gd)),
            pl.BlockSpec((None, MEM_LEN, MEM_WIDTH), lambda bi, i: (bi, 0, 0)),
            pl.BlockSpec((None, MEM_LEN, MEM_WIDTH), lambda bi, i: (bi, 0, 0)),
            _const_spec((ATTN_WIDTH, D_MODEL)),
            _const_spec((GMLP_WIDTH, D_MODEL)),
            _const_spec((MEM_WIDTH, D_MODEL)),
            _const_spec((D_MODEL, D_MODEL)),
            _const_spec((1, D_MODEL)),
        ],
        out_specs=pl.BlockSpec((None, tm, D_MODEL), lambda bi, i: (bi, i, 0)),
        compiler_params=pltpu.CompilerParams(
            dimension_semantics=("arbitrary", "arbitrary"), vmem_limit_bytes=VMEM_LIMIT),
        name="mixer",
    )(x3, a4, pre, w_rest, w_bg, b_bg, gvn, ws, bs, km, vm, wpa, wpg, wpm, wo, post)


def _rope_tables(seq):
    rows = seq // GRID_W
    row = jnp.repeat(jnp.arange(rows, dtype=F32), GRID_W)
    col = jnp.tile(jnp.arange(GRID_W, dtype=F32), rows)
    inv_freq = ROPE_THETA ** (-jnp.arange(ROPE_NFREQ, dtype=F32) / ROPE_NFREQ)
    ang = jnp.stack([row[:, None] * inv_freq, col[:, None] * inv_freq], axis=1)
    cos, sin = jnp.cos(ang), jnp.sin(ang)
    cos64 = jnp.concatenate([cos[:, 0], cos[:, 0], cos[:, 1], cos[:, 1]], axis=1)
    sin64 = jnp.concatenate([-sin[:, 0], sin[:, 0], -sin[:, 1], sin[:, 1]], axis=1)
    reps = LANES // HEAD_DIM
    return jnp.tile(cos64, (1, reps)), jnp.tile(sin64, (1, reps))


def _block_ones(width):
    idx = np.arange(width) // HEAD_DIM
    return jnp.asarray((idx[:, None] == idx[None, :]).astype(np.float32), dtype=BF16)


def kernel(x, mem, ffn1_pre, ffn1_w_gate, ffn1_w_up, ffn1_w_down, ffn1_post, mix_pre, mem_norm, w_in, w_mem_kv, q_norm, k_norm, gmlp_v_norm, gmlp_w_s, gmlp_b_s, w_branch_gate, b_branch_gate, w_proj_attn, w_proj_gmlp, w_proj_mem, w_out, mix_post, ffn2_pre, ffn2_w_gate, ffn2_w_up, ffn2_w_down, ffn2_post):
    b, s, d = x.shape
    depth = w_in.shape[0]
    cos_t, sin_t = _rope_tables(s)
    ones_q = _block_ones(ATTN_WIDTH)
    ones_k = _block_ones(KV_WIDTH)
    qkv_w = ATTN_WIDTH + 2 * KV_WIDTH
    gd = GMLP_WIDTH // GMLP_GROUPS
    row = lambda v: v.reshape(1, -1)
    bf = lambda w: w.astype(BF16)

    for l in range(depth):
        x = _ffn(x.reshape(b * s, d), row(ffn1_pre[l]), bf(ffn1_w_gate[l]), bf(ffn1_w_up[l]),
                 bf(ffn1_w_down[l]), row(ffn1_post[l])).reshape(b, s, d)

        qg = row(jnp.tile(q_norm[l], N_Q_HEADS)) * (HEAD_DIM ** -0.5 * np.log2(np.e))
        kg = row(jnp.tile(k_norm[l], N_KV_HEADS))
        q4, kt, v = _qkv(x, row(mix_pre[l]), bf(w_in[l][:, :qkv_w]), qg, kg, ones_q, ones_k,
                         cos_t, sin_t)
        km, vm = _memkv(mem, row(mem_norm[l]), bf(w_mem_kv[l]))
        a4 = _attn(q4, kt, v)
        bs = jnp.broadcast_to(gmlp_b_s[l][:, :, None], (GMLP_GROUPS, GMLP_CHUNK, gd))
        x = _mixer(x, a4, row(mix_pre[l]), bf(w_in[l][:, qkv_w:]), bf(w_branch_gate[l]),
                   row(b_branch_gate[l]), row(gmlp_v_norm[l]), bf(gmlp_w_s[l]), bs, km, vm,
                   bf(w_proj_attn[l]), bf(w_proj_gmlp[l]), bf(w_proj_mem[l]), bf(w_out[l]),
                   row(mix_post[l]))

        x = _ffn(x.reshape(b * s, d), row(ffn2_pre[l]), bf(ffn2_w_gate[l]), bf(ffn2_w_up[l]),
                 bf(ffn2_w_down[l]), row(ffn2_post[l])).reshape(b, s, d)
    return x
```

```python
import functools

import numpy as np
import jax
import jax.numpy as jnp
from jax import lax
from jax.experimental import pallas as pl
from jax.experimental.pallas import tpu as pltpu

D_MODEL = 1024
MEM_LEN = 256
GRID_W = 64
EPS = 1e-6
HEAD_DIM = 64
ATTN_WIDTH = 512
N_Q_HEADS = 8
N_KV_HEADS = 2
KV_WIDTH = 128
ROPE_THETA = 10000.0
ROPE_NFREQ = 16
GMLP_WIDTH = 512
GMLP_GROUPS = 4
GMLP_CHUNK = 128
MEM_HEADS = 4
MEM_HEAD_DIM = 128
MEM_WIDTH = 512
D_FF = 2816

LANES = 128
N_SLABS = ATTN_WIDTH // LANES
VMEM_LIMIT = 56 * 1024 * 1024

FFN_TM = 1024
FFN_SUB = 512
FFN_FC = 256
QKV_TM = 512
ATT_TQ = 512
ATT_KC = 256
ATT_SUM_ROWS = 16
MIX_TM = 512

BF16 = jnp.bfloat16
F32 = jnp.float32


def _rms(x, g):
    ms = jnp.mean(x * x, axis=-1, keepdims=True)
    return x * lax.rsqrt(ms + EPS) * g


def _dot(a, b):
    return jnp.dot(a, b, preferred_element_type=F32)


def _const_spec(shape):
    nd = len(shape)
    return pl.BlockSpec(shape, lambda *_: (0,) * nd, pipeline_mode=pl.Buffered(1))


def _ffn_kernel(x_ref, pre_ref, wg_ref, wu_ref, wd_ref, post_ref, o_ref):
    n_sub = x_ref.shape[0] // FFN_SUB
    rows = [slice(r * FFN_SUB, (r + 1) * FFN_SUB) for r in range(n_sub)]
    hs = [_rms(x_ref[r, :], pre_ref[...]).astype(BF16) for r in rows]
    accs = [None] * n_sub
    for c in range(D_FF // FFN_FC):
        sl = slice(c * FFN_FC, (c + 1) * FFN_FC)
        for k in range(n_sub):
            g = _dot(hs[k], wg_ref[:, sl])
            u = _dot(hs[k], wu_ref[:, sl])
            a = (g * jax.nn.sigmoid(g) * u).astype(BF16)
            d = _dot(a, wd_ref[sl, :])
            accs[k] = d if c == 0 else accs[k] + d
    for k, r in enumerate(rows):
        o_ref[r, :] = x_ref[r, :] + 0.5 * _rms(accs[k], post_ref[...])


def _ffn(x2d, pre, wg, wu, wd, post):
    t = x2d.shape[0]
    return pl.pallas_call(
        _ffn_kernel,
        out_shape=jax.ShapeDtypeStruct((t, D_MODEL), F32),
        grid=(t // FFN_TM,),
        in_specs=[
            pl.BlockSpec((FFN_TM, D_MODEL), lambda i: (i, 0)),
            _const_spec((1, D_MODEL)),
            _const_spec((D_MODEL, D_FF)),
            _const_spec((D_MODEL, D_FF)),
            _const_spec((D_FF, D_MODEL)),
            _const_spec((1, D_MODEL)),
        ],
        out_specs=pl.BlockSpec((FFN_TM, D_MODEL), lambda i: (i, 0)),
        compiler_params=pltpu.CompilerParams(
            dimension_semantics=("arbitrary",), vmem_limit_bytes=VMEM_LIMIT),
        name="ffn",
    )(x2d, pre, wg, wu, wd, post)


def _head_norm(z, ones_bd, gain):
    z2 = z * z
    hi = z2.astype(BF16)
    lo = (z2 - hi.astype(F32)).astype(BF16)
    ssum = _dot(hi, ones_bd) + _dot(lo, ones_bd)
    return z * lax.rsqrt(ssum * (1.0 / HEAD_DIM) + EPS) * gain


def _rope(x, cos_t, sin_t, first_half):
    up = pltpu.roll(x, LANES - ROPE_NFREQ, axis=1)
    dn = pltpu.roll(x, ROPE_NFREQ, axis=1)
    return x * cos_t + jnp.where(first_half, up, dn) * sin_t


def _qkv_kernel(x_ref, pre_ref, w_ref, qg_ref, kg_ref, onesq_ref, onesk_ref,
                cos_ref, sin_ref, qt_ref, k_ref, vt_ref):
    h = _rms(x_ref[...], pre_ref[...]).astype(BF16)
    z = _dot(h, w_ref[...])
    cos_t = cos_ref[...]
    sin_t = sin_ref[...]
    lane = lax.broadcasted_iota(jnp.int32, cos_t.shape, 1)
    first_half = (lane % (2 * ROPE_NFREQ)) < ROPE_NFREQ
    qn = _head_norm(z[:, :ATTN_WIDTH], onesq_ref[...], qg_ref[...])
    for j in range(N_SLABS):
        qr = _rope(qn[:, j * LANES:(j + 1) * LANES], cos_t, sin_t, first_half)
        qt_ref[j] = qr.T.astype(BF16)
    kn = _head_norm(z[:, ATTN_WIDTH:ATTN_WIDTH + KV_WIDTH], onesk_ref[...], kg_ref[...])
    kr = _rope(kn, cos_t, sin_t, first_half)
    k_sw = pltpu.roll(kr, HEAD_DIM, axis=1)
    lo_half = lane < HEAD_DIM
    k_ref[0] = jnp.where(lo_half, kr, k_sw).astype(BF16)
    k_ref[1] = jnp.where(lo_half, k_sw, kr).astype(BF16)
    vt_ref[...] = z[:, ATTN_WIDTH + KV_WIDTH:].T.astype(BF16)


def _qkv(x3, pre, w_qkv, qg, kg, ones_q, ones_k, cos_t, sin_t):
    b, s, _ = x3.shape
    tm = QKV_TM
    return pl.pallas_call(
        _qkv_kernel,
        out_shape=(
            jax.ShapeDtypeStruct((b, N_SLABS, LANES, s), BF16),
            jax.ShapeDtypeStruct((b, N_KV_HEADS, s, LANES), BF16),
            jax.ShapeDtypeStruct((b, KV_WIDTH, s), BF16),
        ),
        grid=(b, s // tm),
        in_specs=[
            pl.BlockSpec((None, tm, D_MODEL), lambda bi, i: (bi, i, 0)),
            _const_spec((1, D_MODEL)),
            _const_spec((D_MODEL, ATTN_WIDTH + 2 * KV_WIDTH)),
            _const_spec((1, ATTN_WIDTH)),
            _const_spec((1, KV_WIDTH)),
            _const_spec((ATTN_WIDTH, ATTN_WIDTH)),
            _const_spec((KV_WIDTH, KV_WIDTH)),
            pl.BlockSpec((tm, LANES), lambda bi, i: (i, 0)),
            pl.BlockSpec((tm, LANES), lambda bi, i: (i, 0)),
        ],
        out_specs=(
            pl.BlockSpec((None, N_SLABS, LANES, tm), lambda bi, i: (bi, 0, 0, i)),
            pl.BlockSpec((None, N_KV_HEADS, tm, LANES), lambda bi, i: (bi, 0, i, 0)),
            pl.BlockSpec((None, KV_WIDTH, tm), lambda bi, i: (bi, 0, i)),
        ),
        compiler_params=pltpu.CompilerParams(
            dimension_semantics=("arbitrary", "arbitrary"), vmem_limit_bytes=VMEM_LIMIT),
        name="qkv",
    )(x3, pre, w_qkv, qg, kg, ones_q, ones_k, cos_t, sin_t)


def _memkv_kernel(m_ref, g_ref, w_ref, k_ref, v_ref):
    mn = _rms(m_ref[...], g_ref[...]).astype(BF16)
    kv = _dot(mn, w_ref[...])
    k_ref[...] = kv[:, :MEM_WIDTH].astype(BF16)
    v_ref[...] = kv[:, MEM_WIDTH:].astype(BF16)


def _memkv(mem, g, w):
    b = mem.shape[0]
    return pl.pallas_call(
        _memkv_kernel,
        out_shape=(
            jax.ShapeDtypeStruct((b, MEM_LEN, MEM_WIDTH), BF16),
            jax.ShapeDtypeStruct((b, MEM_LEN, MEM_WIDTH), BF16),
        ),
        grid=(b,),
        in_specs=[
            pl.BlockSpec((None, MEM_LEN, D_MODEL), lambda bi: (bi, 0, 0)),
            _const_spec((1, D_MODEL)),
            _const_spec((D_MODEL, 2 * MEM_WIDTH)),
        ],
        out_specs=(
            pl.BlockSpec((None, MEM_LEN, MEM_WIDTH), lambda bi: (bi, 0, 0)),
            pl.BlockSpec((None, MEM_LEN, MEM_WIDTH), lambda bi: (bi, 0, 0)),
        ),
        compiler_params=pltpu.CompilerParams(dimension_semantics=("arbitrary",)),
        name="memkv",
    )(mem, g, w)


def _attn_kernel(qt_ref, k_ref, vt_ref, o_ref):
    tq = qt_ref.shape[2]
    seq = k_ref.shape[0]
    zeros = jnp.zeros((HEAD_DIM, tq), BF16)
    ones = jnp.ones((ATT_SUM_ROWS, ATT_KC), BF16)
    n_heads = 2 * qt_ref.shape[0]
    n_chunks = seq // ATT_KC
    rhs = []
    for head in range(n_heads):
        j, par = divmod(head, 2)
        qh = qt_ref[j, par * HEAD_DIM:(par + 1) * HEAD_DIM, :]
        rhs.append(jnp.concatenate([qh, zeros] if par == 0 else [zeros, qh], axis=0))
    m = [None] * n_heads
    acc = [None] * n_heads
    s_prev = None
    for c in range(n_chunks + 1):
        s_cur = None
        if c < n_chunks:
            ksl = slice(c * ATT_KC, (c + 1) * ATT_KC)
            s_cur = [_dot(k_ref[ksl, :], rhs[h]) for h in range(n_heads)]
        if c > 0:
            vsl = slice((c - 1) * ATT_KC, c * ATT_KC)
            v_ext = jnp.concatenate([vt_ref[:, vsl], ones], axis=0)
            for h in range(n_heads):
                s = s_prev[h]
                mc = jnp.max(s, axis=0, keepdims=True)
                if c == 1:
                    m[h] = mc
                    acc[h] = _dot(v_ext, jnp.exp2(s - mc).astype(BF16))
                else:
                    m_new = jnp.maximum(m[h], mc)
                    acc[h] = (jnp.exp2(m[h] - m_new) * acc[h]
                              + _dot(v_ext, jnp.exp2(s - m_new).astype(BF16)))
                    m[h] = m_new
        s_prev = s_cur
    for h in range(n_heads):
        o = acc[h][:HEAD_DIM] / acc[h][HEAD_DIM:HEAD_DIM + 1]
        o_ref[h * HEAD_DIM:(h + 1) * HEAD_DIM, :] = o.astype(BF16)


def _attn(qt, k2, vt):
    b, _, _, s = qt.shape
    tq = ATT_TQ
    per_grp = N_SLABS // N_KV_HEADS
    return pl.pallas_call(
        _attn_kernel,
        out_shape=jax.ShapeDtypeStruct((b, ATTN_WIDTH, s), BF16),
        grid=(b, N_KV_HEADS, s // tq),
        in_specs=[
            pl.BlockSpec((None, per_grp, LANES, tq), lambda bi, g, i: (bi, g, 0, i)),
            pl.BlockSpec((None, None, s, LANES), lambda bi, g, i: (bi, g, 0, 0)),
            pl.BlockSpec((None, HEAD_DIM, s), lambda bi, g, i: (bi, g, 0)),
        ],
        out_specs=pl.BlockSpec((None, per_grp * LANES, tq), lambda bi, g, i: (bi, g, i)),
        compiler_params=pltpu.CompilerParams(
            dimension_semantics=("arbitrary", "arbitrary", "arbitrary"),
            vmem_limit_bytes=VMEM_LIMIT),
        name="attn",
    )(qt, k2, vt)


def _mixer_kernel(x_ref, a_ref, pre_ref, wr_ref, wbg_ref, bbg_ref, gvn_ref, ws_ref, bs_ref,
                  km_ref, vm_ref, wpa_ref, wpg_ref, wpm_ref, wo_ref, post_ref, o_ref):
    x = x_ref[...]
    tm = x.shape[0]
    nch = tm // GMLP_CHUNK
    gd = GMLP_WIDTH // GMLP_GROUPS
    hb = _rms(x, pre_ref[...]).astype(BF16)
    z = _dot(hb, wr_ref[...])
    gu = jax.nn.gelu(z[:, :GMLP_WIDTH])
    gv = jax.nn.gelu(z[:, GMLP_WIDTH:2 * GMLP_WIDTH])
    qm = z[:, 2 * GMLP_WIDTH:].astype(BF16)

    vn = _rms(gv, gvn_ref[...]).astype(BF16)
    cols = []
    for g in range(GMLP_GROUPS):
        rhs = jnp.concatenate(
            [vn[c * GMLP_CHUNK:(c + 1) * GMLP_CHUNK, g * gd:(g + 1) * gd] for c in range(nch)], axis=1)
        mixed = _dot(ws_ref[g], rhs) + jnp.concatenate([bs_ref[g]] * nch, axis=1)
        cols.append(jnp.concatenate(
            [mixed[:, c * gd:(c + 1) * gd] for c in range(nch)], axis=0))
    gm = (gu * jnp.concatenate(cols, axis=1)).astype(BF16)

    outs = []
    for hh in range(MEM_HEADS):
        sl = slice(hh * MEM_HEAD_DIM, (hh + 1) * MEM_HEAD_DIM)
        s = lax.dot_general(qm[:, sl], km_ref[:, sl], (((1,), (1,)), ((), ())),
                            preferred_element_type=F32) * (MEM_HEAD_DIM ** -0.5)
        m = jnp.max(s, axis=-1, keepdims=True)
        p = jnp.exp(s - m)
        l = jnp.sum(p, axis=-1, keepdims=True)
        outs.append(_dot(p.astype(BF16), vm_ref[:, sl]) / l)
    mo = jnp.concatenate(outs, axis=1).astype(BF16)

    br_attn = lax.dot_general(a_ref[...], wpa_ref[...], (((0,), (0,)), ((), ())),
                              preferred_element_type=F32)
    gates = jax.nn.sigmoid(_dot(hb, wbg_ref[...]) + bbg_ref[...])
    merged = (gates[:, :D_MODEL] * br_attn
              + gates[:, D_MODEL:2 * D_MODEL] * _dot(gm, wpg_ref[...])
              + gates[:, 2 * D_MODEL:] * _dot(mo, wpm_ref[...]))
    out = _dot(merged.astype(BF16), wo_ref[...])
    o_ref[...] = x + _rms(out, post_ref[...])


def _mixer(x3, a4, pre, w_rest, w_bg, b_bg, gvn, ws, bs, km, vm, wpa, wpg, wpm, wo, post):
    b, s, _ = x3.shape
    tm = MIX_TM
    gd = GMLP_WIDTH // GMLP_GROUPS
    return pl.pallas_call(
        _mixer_kernel,
        out_shape=jax.ShapeDtypeStruct((b, s, D_MODEL), F32),
        grid=(b, s // tm),
        in_specs=[
            pl.BlockSpec((None, tm, D_MODEL), lambda bi, i: (bi, i, 0)),
            pl.BlockSpec((None, ATTN_WIDTH, tm), lambda bi, i: (bi, 0, i)),
            _const_spec((1, D_MODEL)),
            _const_spec((D_MODEL, 2 * GMLP_WIDTH + MEM_WIDTH)),
            _const_spec((D_MODEL, 3 * D_MODEL)),
            _const_spec((1, 3 * D_MODEL)),
            _const_spec((1, GMLP_WIDTH)),
            _const_spec((GMLP_GROUPS, GMLP_CHUNK, GMLP_CHUNK)),
            _const_spec((GMLP_GROUPS, GMLP_CHUNK, gd)),
            pl.BlockSpec((None, MEM_LEN, MEM_WIDTH), lambda bi, i: (bi, 0, 0)),
            pl.BlockSpec((None, MEM_LEN, MEM_WIDTH), lambda bi, i: (bi, 0, 0)),
            _const_spec((ATTN_WIDTH, D_MODEL)),
            _const_spec((GMLP_WIDTH, D_MODEL)),
            _const_spec((MEM_WIDTH, D_MODEL)),
            _const_spec((D_MODEL, D_MODEL)),
            _const_spec((1, D_MODEL)),
        ],
        out_specs=pl.BlockSpec((None, tm, D_MODEL), lambda bi, i: (bi, i, 0)),
        compiler_params=pltpu.CompilerParams(
            dimension_semantics=("arbitrary", "arbitrary"), vmem_limit_bytes=VMEM_LIMIT),
        name="mixer",
    )(x3, a4, pre, w_rest, w_bg, b_bg, gvn, ws, bs, km, vm, wpa, wpg, wpm, wo, post)


def _rope_tables(seq):
    rows = seq // GRID_W
    row = jnp.repeat(jnp.arange(rows, dtype=F32), GRID_W)
    col = jnp.tile(jnp.arange(GRID_W, dtype=F32), rows)
    inv_freq = ROPE_THETA ** (-jnp.arange(ROPE_NFREQ, dtype=F32) / ROPE_NFREQ)
    ang = jnp.stack([row[:, None] * inv_freq, col[:, None] * inv_freq], axis=1)
    cos, sin = jnp.cos(ang), jnp.sin(ang)
    cos64 = jnp.concatenate([cos[:, 0], cos[:, 0], cos[:, 1], cos[:, 1]], axis=1)
    sin64 = jnp.concatenate([-sin[:, 0], sin[:, 0], -sin[:, 1], sin[:, 1]], axis=1)
    reps = LANES // HEAD_DIM
    return jnp.tile(cos64, (1, reps)), jnp.tile(sin64, (1, reps))


def _block_ones(width):
    idx = np.arange(width) // HEAD_DIM
    return jnp.asarray((idx[:, None] == idx[None, :]).astype(np.float32), dtype=BF16)


def kernel(x, mem, ffn1_pre, ffn1_w_gate, ffn1_w_up, ffn1_w_down, ffn1_post, mix_pre, mem_norm, w_in, w_mem_kv, q_norm, k_norm, gmlp_v_norm, gmlp_w_s, gmlp_b_s, w_branch_gate, b_branch_gate, w_proj_attn, w_proj_gmlp, w_proj_mem, w_out, mix_post, ffn2_pre, ffn2_w_gate, ffn2_w_up, ffn2_w_down, ffn2_post):
    b, s, d = x.shape
    depth = w_in.shape[0]
    cos_t, sin_t = _rope_tables(s)
    ones_q = _block_ones(ATTN_WIDTH)
    ones_k = _block_ones(KV_WIDTH)
    qkv_w = ATTN_WIDTH + 2 * KV_WIDTH
    gd = GMLP_WIDTH // GMLP_GROUPS
    row = lambda v: v.reshape(1, -1)
    bf = lambda w: w.astype(BF16)

    for l in range(depth):
        x = _ffn(x.reshape(b * s, d), row(ffn1_pre[l]), bf(ffn1_w_gate[l]), bf(ffn1_w_up[l]),
                 bf(ffn1_w_down[l]), row(ffn1_post[l])).reshape(b, s, d)

        qg = row(jnp.tile(q_norm[l], N_Q_HEADS)) * (HEAD_DIM ** -0.5 * np.log2(np.e))
        kg = row(jnp.tile(k_norm[l], N_KV_HEADS))
        qt, k2, vt = _qkv(x, row(mix_pre[l]), bf(w_in[l][:, :qkv_w]), qg, kg, ones_q, ones_k,
                          cos_t, sin_t)
        km, vm = _memkv(mem, row(mem_norm[l]), bf(w_mem_kv[l]))
        a4 = _attn(qt, k2, vt)
        bs = jnp.broadcast_to(gmlp_b_s[l][:, :, None], (GMLP_GROUPS, GMLP_CHUNK, gd))
        x = _mixer(x, a4, row(mix_pre[l]), bf(w_in[l][:, qkv_w:]), bf(w_branch_gate[l]),
                   row(b_branch_gate[l]), row(gmlp_v_norm[l]), bf(gmlp_w_s[l]), bs, km, vm,
                   bf(w_proj_attn[l]), bf(w_proj_gmlp[l]), bf(w_proj_mem[l]), bf(w_out[l]),
                   row(mix_post[l]))

        x = _ffn(x.reshape(b * s, d), row(ffn2_pre[l]), bf(ffn2_w_gate[l]), bf(ffn2_w_up[l]),
                 bf(ffn2_w_down[l]), row(ffn2_post[l])).reshape(b, s, d)
    return x
```

```python
import functools

import numpy as np
import jax
import jax.numpy as jnp
from jax import lax
from jax.experimental import pallas as pl
from jax.experimental.pallas import tpu as pltpu

D_MODEL = 1024
MEM_LEN = 256
GRID_W = 64
EPS = 1e-6
HEAD_DIM = 64
ATTN_WIDTH = 512
N_Q_HEADS = 8
N_KV_HEADS = 2
KV_WIDTH = 128
ROPE_THETA = 10000.0
ROPE_NFREQ = 16
GMLP_WIDTH = 512
GMLP_GROUPS = 4
GMLP_CHUNK = 128
MEM_HEADS = 4
MEM_HEAD_DIM = 128
MEM_WIDTH = 512
D_FF = 2816

LANES = 128
N_SLABS = ATTN_WIDTH // LANES
VMEM_LIMIT = 56 * 1024 * 1024

FFN_TM = 1024
FFN_SUB = 512
FFN_FC = 256
QKV_TM = 512
ATT_TQ = 512
ATT_KC = 256
MIX_TM = 512

BF16 = jnp.bfloat16
F32 = jnp.float32


def _rms(x, g):
    ms = jnp.mean(x * x, axis=-1, keepdims=True)
    return x * lax.rsqrt(ms + EPS) * g


def _dot(a, b):
    return jnp.dot(a, b, preferred_element_type=F32)


def _const_spec(shape):
    nd = len(shape)
    return pl.BlockSpec(shape, lambda *_: (0,) * nd, pipeline_mode=pl.Buffered(1))


def _ffn_kernel(layer, x_ref, pre_ref, wg_hbm, wu_hbm, wd_hbm, post_ref, o_ref,
                wg_ref, wu_ref, wd_ref, sg_ref, su_ref, sd_ref, sem, h_ref, acc_ref):
    n_chunks = D_FF // FFN_FC
    n_sub = x_ref.shape[0] // FFN_SUB
    rows = [slice(r * FFN_SUB, (r + 1) * FFN_SUB) for r in range(n_sub)]

    def chunk_copies(c, slot):
        sl = pl.ds(c * FFN_FC, FFN_FC)
        return (
            pltpu.make_async_copy(wg_hbm.at[layer, :, sl], sg_ref.at[slot], sem.at[0, slot]),
            pltpu.make_async_copy(wu_hbm.at[layer, :, sl], su_ref.at[slot], sem.at[1, slot]),
            pltpu.make_async_copy(wd_hbm.at[layer, sl, :], sd_ref.at[slot], sem.at[2, slot]),
        )

    def body(stage_weights):
        if stage_weights:
            for cp in chunk_copies(0, 0):
                cp.start()
        for r in rows:
            h_ref[r, :] = _rms(x_ref[r, :], pre_ref[...]).astype(BF16)
        for c in range(n_chunks):
            sl = slice(c * FFN_FC, (c + 1) * FFN_FC)
            if stage_weights:
                slot = c % 2
                if c + 1 < n_chunks:
                    for cp in chunk_copies(c + 1, 1 - slot):
                        cp.start()
                for cp in chunk_copies(c, slot):
                    cp.wait()
                wg_ref[:, sl] = sg_ref[slot].astype(BF16)
                wu_ref[:, sl] = su_ref[slot].astype(BF16)
                wd_ref[sl, :] = sd_ref[slot].astype(BF16)
            for r in rows:
                h = h_ref[r, :]
                g = _dot(h, wg_ref[:, sl])
                u = _dot(h, wu_ref[:, sl])
                a = (g * jax.nn.sigmoid(g) * u).astype(BF16)
                d = _dot(a, wd_ref[sl, :])
                if c == 0:
                    acc_ref[r, :] = d
                else:
                    acc_ref[r, :] += d
        for r in rows:
            o_ref[r, :] = x_ref[r, :] + 0.5 * _rms(acc_ref[r, :], post_ref[...])

    first = pl.program_id(0) == 0
    pl.when(first)(functools.partial(body, True))
    pl.when(jnp.logical_not(first))(functools.partial(body, False))


def _ffn(x2d, layer, pre, wg, wu, wd, post):
    t = x2d.shape[0]
    any_spec = pl.BlockSpec(memory_space=pl.ANY)
    return pl.pallas_call(
        functools.partial(_ffn_kernel, layer),
        out_shape=jax.ShapeDtypeStruct((t, D_MODEL), F32),
        grid=(t // FFN_TM,),
        in_specs=[
            pl.BlockSpec((FFN_TM, D_MODEL), lambda i: (i, 0)),
            _const_spec((1, D_MODEL)),
            any_spec,
            any_spec,
            any_spec,
            _const_spec((1, D_MODEL)),
        ],
        out_specs=pl.BlockSpec((FFN_TM, D_MODEL), lambda i: (i, 0)),
        scratch_shapes=[
            pltpu.VMEM((D_MODEL, D_FF), BF16),
            pltpu.VMEM((D_MODEL, D_FF), BF16),
            pltpu.VMEM((D_FF, D_MODEL), BF16),
            pltpu.VMEM((2, D_MODEL, FFN_FC), F32),
            pltpu.VMEM((2, D_MODEL, FFN_FC), F32),
            pltpu.VMEM((2, FFN_FC, D_MODEL), F32),
            pltpu.SemaphoreType.DMA((3, 2)),
            pltpu.VMEM((FFN_TM, D_MODEL), BF16),
            pltpu.VMEM((FFN_TM, D_MODEL), F32),
        ],
        compiler_params=pltpu.CompilerParams(
            dimension_semantics=("arbitrary",), vmem_limit_bytes=VMEM_LIMIT),
        name="ffn",
    )(x2d, pre, wg, wu, wd, post)


def _head_norm(z, ones_bd, gain):
    z2 = z * z
    hi = z2.astype(BF16)
    lo = (z2 - hi.astype(F32)).astype(BF16)
    ssum = _dot(hi, ones_bd) + _dot(lo, ones_bd)
    return z * lax.rsqrt(ssum * (1.0 / HEAD_DIM) + EPS) * gain


def _rope(x, cos_t, sin_t, first_half):
    up = pltpu.roll(x, LANES - ROPE_NFREQ, axis=1)
    dn = pltpu.roll(x, ROPE_NFREQ, axis=1)
    return x * cos_t + jnp.where(first_half, up, dn) * sin_t


def _qkv_kernel(x_ref, pre_ref, w_ref, qg_ref, kg_ref, onesq_ref, onesk_ref,
                cos_ref, sin_ref, q_ref, kt_ref, v_ref):
    h = _rms(x_ref[...], pre_ref[...]).astype(BF16)
    z = _dot(h, w_ref[...])
    cos_t = cos_ref[...]
    sin_t = sin_ref[...]
    lane = lax.broadcasted_iota(jnp.int32, cos_t.shape, 1)
    first_half = (lane % (2 * ROPE_NFREQ)) < ROPE_NFREQ
    qn = _head_norm(z[:, :ATTN_WIDTH], onesq_ref[...], qg_ref[...])
    for j in range(N_SLABS):
        q_ref[j] = _rope(qn[:, j * LANES:(j + 1) * LANES], cos_t, sin_t, first_half).astype(BF16)
    kn = _head_norm(z[:, ATTN_WIDTH:ATTN_WIDTH + KV_WIDTH], onesk_ref[...], kg_ref[...])
    kr = _rope(kn, cos_t, sin_t, first_half)
    kt = kr.T.astype(BF16)
    kt_ref[0] = kt[:HEAD_DIM]
    kt_ref[1] = kt[HEAD_DIM:]
    v_ref[...] = z[:, ATTN_WIDTH + KV_WIDTH:].astype(BF16)


def _qkv(x3, pre, w_qkv, qg, kg, ones_q, ones_k, cos_t, sin_t):
    b, s, _ = x3.shape
    tm = QKV_TM
    return pl.pallas_call(
        _qkv_kernel,
        out_shape=(
            jax.ShapeDtypeStruct((b, N_SLABS, s, LANES), BF16),
            jax.ShapeDtypeStruct((b, N_KV_HEADS, HEAD_DIM, s), BF16),
            jax.ShapeDtypeStruct((b, s, KV_WIDTH), BF16),
        ),
        grid=(b, s // tm),
        in_specs=[
            pl.BlockSpec((None, tm, D_MODEL), lambda bi, i: (bi, i, 0)),
            _const_spec((1, D_MODEL)),
            _const_spec((D_MODEL, ATTN_WIDTH + 2 * KV_WIDTH)),
            _const_spec((1, ATTN_WIDTH)),
            _const_spec((1, KV_WIDTH)),
            _const_spec((ATTN_WIDTH, ATTN_WIDTH)),
            _const_spec((KV_WIDTH, KV_WIDTH)),
            pl.BlockSpec((tm, LANES), lambda bi, i: (i, 0)),
            pl.BlockSpec((tm, LANES), lambda bi, i: (i, 0)),
        ],
        out_specs=(
            pl.BlockSpec((None, N_SLABS, tm, LANES), lambda bi, i: (bi, 0, i, 0)),
            pl.BlockSpec((None, N_KV_HEADS, HEAD_DIM, tm), lambda bi, i: (bi, 0, 0, i)),
            pl.BlockSpec((None, tm, KV_WIDTH), lambda bi, i: (bi, i, 0)),
        ),
        compiler_params=pltpu.CompilerParams(
            dimension_semantics=("arbitrary", "arbitrary"), vmem_limit_bytes=VMEM_LIMIT),
        name="qkv",
    )(x3, pre, w_qkv, qg, kg, ones_q, ones_k, cos_t, sin_t)


def _memkv_kernel(m_ref, g_ref, w_ref, k_ref, v_ref):
    mn = _rms(m_ref[...], g_ref[...]).astype(BF16)
    kv = _dot(mn, w_ref[...])
    k_ref[...] = kv[:, :MEM_WIDTH].astype(BF16)
    v_ref[...] = kv[:, MEM_WIDTH:].astype(BF16)


def _memkv(mem, g, w):
    b = mem.shape[0]
    return pl.pallas_call(
        _memkv_kernel,
        out_shape=(
            jax.ShapeDtypeStruct((b, MEM_LEN, MEM_WIDTH), BF16),
            jax.ShapeDtypeStruct((b, MEM_LEN, MEM_WIDTH), BF16),
        ),
        grid=(b,),
        in_specs=[
            pl.BlockSpec((None, MEM_LEN, D_MODEL), lambda bi: (bi, 0, 0)),
            _const_spec((1, D_MODEL)),
            _const_spec((D_MODEL, 2 * MEM_WIDTH)),
        ],
        out_specs=(
            pl.BlockSpec((None, MEM_LEN, MEM_WIDTH), lambda bi: (bi, 0, 0)),
            pl.BlockSpec((None, MEM_LEN, MEM_WIDTH), lambda bi: (bi, 0, 0)),
        ),
        compiler_params=pltpu.CompilerParams(dimension_semantics=("arbitrary",)),
        name="memkv",
    )(mem, g, w)


def _attn_kernel(q_ref, kt_ref, v_ref, o_ref):
    tq = q_ref.shape[1]
    seq = v_ref.shape[0]
    lane = lax.broadcasted_iota(jnp.int32, (tq, LANES), 1)
    lo_half = lane < HEAD_DIM
    ones = jnp.ones((ATT_KC, LANES), BF16)
    grp = pl.program_id(1)
    for j in range(q_ref.shape[0]):
        q = q_ref[j]
        zero = jnp.zeros_like(q)
        halves = []
        for par in range(2):
            lhs = jnp.where(lo_half, q, zero) if par == 0 else jnp.where(lo_half, zero, q)
            m = acc = None
            for c in range(seq // ATT_KC):
                ksl = slice(c * ATT_KC, (c + 1) * ATT_KC)
                kt = kt_ref[:, ksl]
                s = _dot(lhs, jnp.concatenate([kt, kt], axis=0))
                v_ext = jnp.concatenate([v_ref[ksl, :], ones], axis=1)
                mc = jnp.max(s, axis=-1, keepdims=True)
                if c == 0:
                    m = mc
                    acc = _dot(jnp.exp2(s - m).astype(BF16), v_ext)
                else:
                    m_new = jnp.maximum(m, mc)
                    acc = jnp.exp2(m - m_new) * acc + _dot(jnp.exp2(s - m_new).astype(BF16), v_ext)
                    m = m_new
            o = acc[:, :LANES] / acc[:, LANES:]
            o_sw = pltpu.roll(o, HEAD_DIM, axis=1)
            halves.append(jnp.where(grp == par, o, o_sw))
        o_ref[j] = jnp.where(lo_half, halves[0], halves[1]).astype(BF16)


def _attn(q4, kt, v):
    b, _, s, _ = q4.shape
    tq = ATT_TQ
    per_grp = N_SLABS // N_KV_HEADS
    return pl.pallas_call(
        _attn_kernel,
        out_shape=jax.ShapeDtypeStruct((b, N_SLABS, s, LANES), BF16),
        grid=(b, N_KV_HEADS, s // tq),
        in_specs=[
            pl.BlockSpec((None, per_grp, tq, LANES), lambda bi, g, i: (bi, g, i, 0)),
            pl.BlockSpec((None, None, HEAD_DIM, s), lambda bi, g, i: (bi, g, 0, 0)),
            pl.BlockSpec((None, s, KV_WIDTH), lambda bi, g, i: (bi, 0, 0)),
        ],
        out_specs=pl.BlockSpec((None, per_grp, tq, LANES), lambda bi, g, i: (bi, g, i, 0)),
        compiler_params=pltpu.CompilerParams(
            dimension_semantics=("arbitrary", "arbitrary", "arbitrary"),
            vmem_limit_bytes=VMEM_LIMIT),
        name="attn",
    )(q4, kt, v)


def _mixer_kernel(x_ref, a_ref, pre_ref, wr_ref, wbg_ref, bbg_ref, gvn_ref, ws_ref, bs_ref,
                  km_ref, vm_ref, wpa_ref, wpg_ref, wpm_ref, wo_ref, post_ref, o_ref):
    x = x_ref[...]
    tm = x.shape[0]
    nch = tm // GMLP_CHUNK
    gd = GMLP_WIDTH // GMLP_GROUPS
    hb = _rms(x, pre_ref[...]).astype(BF16)
    z = _dot(hb, wr_ref[...])
    gu = jax.nn.gelu(z[:, :GMLP_WIDTH])
    gv = jax.nn.gelu(z[:, GMLP_WIDTH:2 * GMLP_WIDTH])
    qm = z[:, 2 * GMLP_WIDTH:].astype(BF16)

    vn = _rms(gv, gvn_ref[...]).astype(BF16)
    cols = []
    for g in range(GMLP_GROUPS):
        rhs = jnp.concatenate(
            [vn[c * GMLP_CHUNK:(c + 1) * GMLP_CHUNK, g * gd:(g + 1) * gd] for c in range(nch)], axis=1)
        mixed = _dot(ws_ref[g], rhs) + jnp.concatenate([bs_ref[g]] * nch, axis=1)
        cols.append(jnp.concatenate(
            [mixed[:, c * gd:(c + 1) * gd] for c in range(nch)], axis=0))
    gm = (gu * jnp.concatenate(cols, axis=1)).astype(BF16)

    outs = []
    for hh in range(MEM_HEADS):
        sl = slice(hh * MEM_HEAD_DIM, (hh + 1) * MEM_HEAD_DIM)
        s = lax.dot_general(qm[:, sl], km_ref[:, sl], (((1,), (1,)), ((), ())),
                            preferred_element_type=F32) * (MEM_HEAD_DIM ** -0.5)
        m = jnp.max(s, axis=-1, keepdims=True)
        p = jnp.exp(s - m)
        l = jnp.sum(p, axis=-1, keepdims=True)
        outs.append(_dot(p.astype(BF16), vm_ref[:, sl]) / l)
    mo = jnp.concatenate(outs, axis=1).astype(BF16)

    att = jnp.concatenate([a_ref[j] for j in range(N_SLABS)], axis=1)
    gates = jax.nn.sigmoid(_dot(hb, wbg_ref[...]) + bbg_ref[...])
    merged = (gates[:, :D_MODEL] * _dot(att, wpa_ref[...])
              + gates[:, D_MODEL:2 * D_MODEL] * _dot(gm, wpg_ref[...])
              + gates[:, 2 * D_MODEL:] * _dot(mo, wpm_ref[...]))
    out = _dot(merged.astype(BF16), wo_ref[...])
    o_ref[...] = x + _rms(out, post_ref[...])


def _mixer(x3, a4, pre, w_rest, w_bg, b_bg, gvn, ws, bs, km, vm, wpa, wpg, wpm, wo, post):
    b, s, _ = x3.shape
    tm = MIX_TM
    gd = GMLP_WIDTH // GMLP_GROUPS
    return pl.pallas_call(
        _mixer_kernel,
        out_shape=jax.ShapeDtypeStruct((b, s, D_MODEL), F32),
        grid=(b, s // tm),
        in_specs=[
            pl.BlockSpec((None, tm, D_MODEL), lambda bi, i: (bi, i, 0)),
            pl.BlockSpec((None, N_SLABS, tm, LANES), lambda bi, i: (bi, 0, i, 0)),
            _const_spec((1, D_MODEL)),
            _const_spec((D_MODEL, 2 * GMLP_WIDTH + MEM_WIDTH)),
            _const_spec((D_MODEL, 3 * D_MODEL)),
            _const_spec((1, 3 * D_MODEL)),
            _const_spec((1, GMLP_WIDTH)),
            _const_spec((GMLP_GROUPS, GMLP_CHUNK, GMLP_CHUNK)),
            _const_spec((GMLP_GROUPS, GMLP_CHUNK, gd)),
            pl.BlockSpec((None, MEM_LEN, MEM_WIDTH), lambda bi, i: (bi, 0, 0)),
            pl.BlockSpec((None, MEM_LEN, MEM_WIDTH), lambda bi, i: (bi, 0, 0)),
            _const_spec((ATTN_WIDTH, D_MODEL)),
            _const_spec((GMLP_WIDTH, D_MODEL)),
            _const_spec((MEM_WIDTH, D_MODEL)),
            _const_spec((D_MODEL, D_MODEL)),
            _const_spec((1, D_MODEL)),
        ],
        out_specs=pl.BlockSpec((None, tm, D_MODEL), lambda bi, i: (bi, i, 0)),
        compiler_params=pltpu.CompilerParams(
            dimension_semantics=("arbitrary", "arbitrary"), vmem_limit_bytes=VMEM_LIMIT),
        name="mixer",
    )(x3, a4, pre, w_rest, w_bg, b_bg, gvn, ws, bs, km, vm, wpa, wpg, wpm, wo, post)


def _rope_tables(seq):
    rows = seq // GRID_W
    row = jnp.repeat(jnp.arange(rows, dtype=F32), GRID_W)
    col = jnp.tile(jnp.arange(GRID_W, dtype=F32), rows)
    inv_freq = ROPE_THETA ** (-jnp.arange(ROPE_NFREQ, dtype=F32) / ROPE_NFREQ)
    ang = jnp.stack([row[:, None] * inv_freq, col[:, None] * inv_freq], axis=1)
    cos, sin = jnp.cos(ang), jnp.sin(ang)
    cos64 = jnp.concatenate([cos[:, 0], cos[:, 0], cos[:, 1], cos[:, 1]], axis=1)
    sin64 = jnp.concatenate([-sin[:, 0], sin[:, 0], -sin[:, 1], sin[:, 1]], axis=1)
    reps = LANES // HEAD_DIM
    return jnp.tile(cos64, (1, reps)), jnp.tile(sin64, (1, reps))


def _block_ones(width):
    idx = np.arange(width) // HEAD_DIM
    return jnp.asarray((idx[:, None] == idx[None, :]).astype(np.float32), dtype=BF16)


def kernel(x, mem, ffn1_pre, ffn1_w_gate, ffn1_w_up, ffn1_w_down, ffn1_post, mix_pre, mem_norm, w_in, w_mem_kv, q_norm, k_norm, gmlp_v_norm, gmlp_w_s, gmlp_b_s, w_branch_gate, b_branch_gate, w_proj_attn, w_proj_gmlp, w_proj_mem, w_out, mix_post, ffn2_pre, ffn2_w_gate, ffn2_w_up, ffn2_w_down, ffn2_post):
    b, s, d = x.shape
    depth = w_in.shape[0]
    cos_t, sin_t = _rope_tables(s)
    ones_q = _block_ones(ATTN_WIDTH)
    ones_k = _block_ones(KV_WIDTH)
    qkv_w = ATTN_WIDTH + 2 * KV_WIDTH
    gd = GMLP_WIDTH // GMLP_GROUPS
    row = lambda v: v.reshape(1, -1)
    bf = lambda w: w.astype(BF16)

    for l in range(depth):
        x = _ffn(x.reshape(b * s, d), l, row(ffn1_pre[l]), ffn1_w_gate, ffn1_w_up, ffn1_w_down,
                 row(ffn1_post[l])).reshape(b, s, d)

        qg = row(jnp.tile(q_norm[l], N_Q_HEADS)) * (HEAD_DIM ** -0.5 * np.log2(np.e))
        kg = row(jnp.tile(k_norm[l], N_KV_HEADS))
        q4, kt, v = _qkv(x, row(mix_pre[l]), bf(w_in[l][:, :qkv_w]), qg, kg, ones_q, ones_k,
                         cos_t, sin_t)
        km, vm = _memkv(mem, row(mem_norm[l]), bf(w_mem_kv[l]))
        a4 = _attn(q4, kt, v)
        bs = jnp.broadcast_to(gmlp_b_s[l][:, :, None], (GMLP_GROUPS, GMLP_CHUNK, gd))
        x = _mixer(x, a4, row(mix_pre[l]), bf(w_in[l][:, qkv_w:]), bf(w_branch_gate[l]),
                   row(b_branch_gate[l]), row(gmlp_v_norm[l]), bf(gmlp_w_s[l]), bs, km, vm,
                   bf(w_proj_attn[l]), bf(w_proj_gmlp[l]), bf(w_proj_mem[l]), bf(w_out[l]),
                   row(mix_post[l]))

        x = _ffn(x.reshape(b * s, d), l, row(ffn2_pre[l]), ffn2_w_gate, ffn2_w_up, ffn2_w_down,
                 row(ffn2_post[l])).reshape(b, s, d)
    return x
```

```python
import functools

import numpy as np
import jax
import jax.numpy as jnp
from jax import lax
from jax.experimental import pallas as pl
from jax.experimental.pallas import tpu as pltpu

D_MODEL = 1024
MEM_LEN = 256
GRID_W = 64
EPS = 1e-6
HEAD_DIM = 64
ATTN_WIDTH = 512
N_Q_HEADS = 8
N_KV_HEADS = 2
KV_WIDTH = 128
QKV_WIDTH = ATTN_WIDTH + 2 * KV_WIDTH
ROPE_THETA = 10000.0
ROPE_NFREQ = 16
GMLP_WIDTH = 512
GMLP_GROUPS = 4
GMLP_CHUNK = 128
MEM_HEADS = 4
MEM_HEAD_DIM = 128
MEM_WIDTH = 512
D_FF = 2816

LANES = 128
BF16_SUBLANES = 16
N_SLABS = ATTN_WIDTH // LANES
VMEM_LIMIT = 56 * 1024 * 1024

FFN_TM = 1024
FFN_SUB = 512
FFN_FC = 256
QKV_TM = 512
ATT_TQ = 512
ATT_KC = 256
MIX_TM = 512

BF16 = jnp.bfloat16
F32 = jnp.float32


def _rms(x, g):
    ms = jnp.mean(x * x, axis=-1, keepdims=True)
    return x * lax.rsqrt(ms + EPS) * g


def _dot(a, b):
    return jnp.dot(a, b, preferred_element_type=F32)


def _const_spec(shape):
    nd = len(shape)
    return pl.BlockSpec(shape, lambda *_: (0,) * nd, pipeline_mode=pl.Buffered(1))


def _ffn_kernel(x_ref, pre_ref, wg_ref, wu_ref, wd_ref, post_ref, o_ref):
    n_sub = x_ref.shape[0] // FFN_SUB
    rows = [slice(r * FFN_SUB, (r + 1) * FFN_SUB) for r in range(n_sub)]
    hs = [_rms(x_ref[r, :], pre_ref[...]).astype(BF16) for r in rows]
    accs = [None] * n_sub
    for c in range(D_FF // FFN_FC):
        sl = slice(c * FFN_FC, (c + 1) * FFN_FC)
        for k in range(n_sub):
            g = _dot(hs[k], wg_ref[:, sl])
            u = _dot(hs[k], wu_ref[:, sl])
            a = (g * jax.nn.sigmoid(g) * u).astype(BF16)
            d = _dot(a, wd_ref[sl, :])
            accs[k] = d if c == 0 else accs[k] + d
    for k, r in enumerate(rows):
        o_ref[r, :] = x_ref[r, :] + 0.5 * _rms(accs[k], post_ref[...])


def _ffn(x2d, pre, wg, wu, wd, post):
    t = x2d.shape[0]
    return pl.pallas_call(
        _ffn_kernel,
        out_shape=jax.ShapeDtypeStruct((t, D_MODEL), F32),
        grid=(t // FFN_TM,),
        in_specs=[
            pl.BlockSpec((FFN_TM, D_MODEL), lambda i: (i, 0)),
            _const_spec((1, D_MODEL)),
            _const_spec((D_MODEL, D_FF)),
            _const_spec((D_MODEL, D_FF)),
            _const_spec((D_FF, D_MODEL)),
            _const_spec((1, D_MODEL)),
        ],
        out_specs=pl.BlockSpec((FFN_TM, D_MODEL), lambda i: (i, 0)),
        compiler_params=pltpu.CompilerParams(
            dimension_semantics=("arbitrary",), vmem_limit_bytes=VMEM_LIMIT),
        name="ffn",
    )(x2d, pre, wg, wu, wd, post)


def _head_norm(z, ones_bd, gain):
    z2 = z * z
    hi = z2.astype(BF16)
    lo = (z2 - hi.astype(F32)).astype(BF16)
    ssum = _dot(hi, ones_bd) + _dot(lo, ones_bd)
    return z * lax.rsqrt(ssum * (1.0 / HEAD_DIM) + EPS) * gain


def _rope(x, cos_t, sin_t, first_half):
    up = pltpu.roll(x, LANES - ROPE_NFREQ, axis=1)
    dn = pltpu.roll(x, ROPE_NFREQ, axis=1)
    return x * cos_t + jnp.where(first_half, up, dn) * sin_t


def _qkv_kernel(x_ref, pre_ref, w_ref, qg_ref, kg_ref, onesq_ref, onesk_ref,
                cos_ref, sin_ref, q_ref, kt_ref, v_ref):
    h = _rms(x_ref[...], pre_ref[...]).astype(BF16)
    z = _dot(h, w_ref[...])
    cos_t = cos_ref[...]
    sin_t = sin_ref[...]
    lane = lax.broadcasted_iota(jnp.int32, cos_t.shape, 1)
    first_half = (lane % (2 * ROPE_NFREQ)) < ROPE_NFREQ
    qn = _head_norm(z[:, :ATTN_WIDTH], onesq_ref[...], qg_ref[...])
    for j in range(N_SLABS):
        q_ref[j] = _rope(qn[:, j * LANES:(j + 1) * LANES], cos_t, sin_t, first_half).astype(BF16)
    kn = _head_norm(z[:, ATTN_WIDTH:ATTN_WIDTH + KV_WIDTH], onesk_ref[...], kg_ref[...])
    kr = _rope(kn, cos_t, sin_t, first_half)
    kt = kr.T.astype(BF16)
    kt_ref[0] = kt[:HEAD_DIM]
    kt_ref[1] = kt[HEAD_DIM:]
    v_ref[...] = z[:, ATTN_WIDTH + KV_WIDTH:].astype(BF16)


def _qkv(x3, pre, w_qkv, qg, kg, ones_q, ones_k, cos_t, sin_t):
    b, s, _ = x3.shape
    tm = QKV_TM
    return pl.pallas_call(
        _qkv_kernel,
        out_shape=(
            jax.ShapeDtypeStruct((b, N_SLABS, s, LANES), BF16),
            jax.ShapeDtypeStruct((b, N_KV_HEADS, HEAD_DIM, s), BF16),
            jax.ShapeDtypeStruct((b, s, KV_WIDTH), BF16),
        ),
        grid=(b, s // tm),
        in_specs=[
            pl.BlockSpec((None, tm, D_MODEL), lambda bi, i: (bi, i, 0)),
            _const_spec((1, D_MODEL)),
            _const_spec((D_MODEL, QKV_WIDTH)),
            _const_spec((1, ATTN_WIDTH)),
            _const_spec((1, KV_WIDTH)),
            _const_spec((ATTN_WIDTH, ATTN_WIDTH)),
            _const_spec((KV_WIDTH, KV_WIDTH)),
            pl.BlockSpec((tm, LANES), lambda bi, i: (i, 0)),
            pl.BlockSpec((tm, LANES), lambda bi, i: (i, 0)),
        ],
        out_specs=(
            pl.BlockSpec((None, N_SLABS, tm, LANES), lambda bi, i: (bi, 0, i, 0)),
            pl.BlockSpec((None, N_KV_HEADS, HEAD_DIM, tm), lambda bi, i: (bi, 0, 0, i)),
            pl.BlockSpec((None, tm, KV_WIDTH), lambda bi, i: (bi, i, 0)),
        ),
        compiler_params=pltpu.CompilerParams(
            dimension_semantics=("arbitrary", "arbitrary"), vmem_limit_bytes=VMEM_LIMIT),
        name="qkv",
    )(x3, pre, w_qkv, qg, kg, ones_q, ones_k, cos_t, sin_t)


def _memkv_kernel(m_ref, g_ref, w_ref, k_ref, v_ref):
    mn = _rms(m_ref[...], g_ref[...]).astype(BF16)
    kv = _dot(mn, w_ref[...])
    k_ref[...] = kv[:, :MEM_WIDTH].astype(BF16)
    v_ref[...] = kv[:, MEM_WIDTH:].astype(BF16)


def _memkv(mem, g, w):
    b = mem.shape[0]
    return pl.pallas_call(
        _memkv_kernel,
        out_shape=(
            jax.ShapeDtypeStruct((b, MEM_LEN, MEM_WIDTH), BF16),
            jax.ShapeDtypeStruct((b, MEM_LEN, MEM_WIDTH), BF16),
        ),
        grid=(b,),
        in_specs=[
            pl.BlockSpec((None, MEM_LEN, D_MODEL), lambda bi: (bi, 0, 0)),
            _const_spec((1, D_MODEL)),
            _const_spec((D_MODEL, 2 * MEM_WIDTH)),
        ],
        out_specs=(
            pl.BlockSpec((None, MEM_LEN, MEM_WIDTH), lambda bi: (bi, 0, 0)),
            pl.BlockSpec((None, MEM_LEN, MEM_WIDTH), lambda bi: (bi, 0, 0)),
        ),
        compiler_params=pltpu.CompilerParams(dimension_semantics=("arbitrary",)),
        name="memkv",
    )(mem, g, w)


def _attn_kernel(cast_steps, q_ref, kt_ref, v_ref, *refs):
    n_cast = len(cast_steps)
    o_ref = refs[n_cast]
    t = ((pl.program_id(0) * pl.num_programs(1) + pl.program_id(1)) * pl.num_programs(2)
         + pl.program_id(2))
    for w_ref, wb_ref, n_steps in zip(refs[:n_cast], refs[n_cast + 1:], cast_steps):
        @pl.when(t < n_steps)
        def _(w_ref=w_ref, wb_ref=wb_ref):
            wb_ref[...] = w_ref[...].astype(BF16)

    tq = q_ref.shape[1]
    seq = v_ref.shape[0]
    lane = lax.broadcasted_iota(jnp.int32, (tq, LANES), 1)
    lo_half = lane < HEAD_DIM
    ones = jnp.ones((ATT_KC, LANES), BF16)
    grp = pl.program_id(1)
    for j in range(q_ref.shape[0]):
        q = q_ref[j]
        zero = jnp.zeros_like(q)
        halves = []
        for par in range(2):
            lhs = jnp.where(lo_half, q, zero) if par == 0 else jnp.where(lo_half, zero, q)
            m = acc = None
            for c in range(seq // ATT_KC):
                ksl = slice(c * ATT_KC, (c + 1) * ATT_KC)
                kt = kt_ref[:, ksl]
                s = _dot(lhs, jnp.concatenate([kt, kt], axis=0))
                v_ext = jnp.concatenate([v_ref[ksl, :], ones], axis=1)
                mc = jnp.max(s, axis=-1, keepdims=True)
                if c == 0:
                    m = mc
                    acc = _dot(jnp.exp2(s - m).astype(BF16), v_ext)
                else:
                    m_new = jnp.maximum(m, mc)
                    acc = jnp.exp2(m - m_new) * acc + _dot(jnp.exp2(s - m_new).astype(BF16), v_ext)
                    m = m_new
            o = acc[:, :LANES] / acc[:, LANES:]
            o_sw = pltpu.roll(o, HEAD_DIM, axis=1)
            halves.append(jnp.where(grp == par, o, o_sw))
        o_ref[j] = jnp.where(lo_half, halves[0], halves[1]).astype(BF16)


def _cast_block_rows(n_rows, n_steps):
    for rows in range(BF16_SUBLANES, n_rows + 1, BF16_SUBLANES):
        if n_rows % rows == 0 and n_rows // rows <= n_steps:
            return rows
    raise ValueError((n_rows, n_steps))


def _attn(q4, kt, v, layer, weights):
    b, _, s, _ = q4.shape
    tq = ATT_TQ
    per_grp = N_SLABS // N_KV_HEADS
    n_tiles = s // tq
    n_grid = b * N_KV_HEADS * n_tiles
    w_in_specs, w_out_specs, w_out_shapes, cast_steps = [], [], [], []
    for w in weights:
        _, n_rows, n_cols = w.shape
        rows = _cast_block_rows(n_rows, n_grid)
        n_steps = n_rows // rows
        cast_steps.append(n_steps)

        def blk(bi, g, i, n_steps=n_steps):
            return jnp.minimum((bi * N_KV_HEADS + g) * n_tiles + i, n_steps - 1)

        w_in_specs.append(pl.BlockSpec((None, rows, n_cols),
                                       lambda bi, g, i, blk=blk: (layer, blk(bi, g, i), 0)))
        w_out_specs.append(pl.BlockSpec((rows, n_cols), lambda bi, g, i, blk=blk: (blk(bi, g, i), 0)))
        w_out_shapes.append(jax.ShapeDtypeStruct((n_rows, n_cols), BF16))
    outs = pl.pallas_call(
        functools.partial(_attn_kernel, tuple(cast_steps)),
        out_shape=[jax.ShapeDtypeStruct((b, N_SLABS, s, LANES), BF16)] + w_out_shapes,
        grid=(b, N_KV_HEADS, n_tiles),
        in_specs=[
            pl.BlockSpec((None, per_grp, tq, LANES), lambda bi, g, i: (bi, g, i, 0)),
            pl.BlockSpec((None, None, HEAD_DIM, s), lambda bi, g, i: (bi, g, 0, 0)),
            pl.BlockSpec((None, s, KV_WIDTH), lambda bi, g, i: (bi, 0, 0)),
        ] + w_in_specs,
        out_specs=[pl.BlockSpec((None, per_grp, tq, LANES), lambda bi, g, i: (bi, g, i, 0))]
        + w_out_specs,
        compiler_params=pltpu.CompilerParams(
            dimension_semantics=("arbitrary", "arbitrary", "arbitrary"),
            vmem_limit_bytes=VMEM_LIMIT),
        name="attn",
    )(q4, kt, v, *weights)
    return outs[0], outs[1:]


def _mixer_kernel(x_ref, a_ref, pre_ref, wra_ref, wrb_ref, wbg_ref, bbg_ref, gvn_ref, ws_ref,
                  bs_ref, km_ref, vm_ref, wpa_ref, wpg_ref, wpm_ref, wo_ref, post_ref, o_ref):
    x = x_ref[...]
    tm = x.shape[0]
    nch = tm // GMLP_CHUNK
    gd = GMLP_WIDTH // GMLP_GROUPS
    hb = _rms(x, pre_ref[...]).astype(BF16)
    z = jnp.concatenate([_dot(hb, wra_ref[...]), _dot(hb, wrb_ref[...])], axis=1)
    gu = jax.nn.gelu(z[:, :GMLP_WIDTH])
    gv = jax.nn.gelu(z[:, GMLP_WIDTH:2 * GMLP_WIDTH])
    qm = z[:, 2 * GMLP_WIDTH:].astype(BF16)

    vn = _rms(gv, gvn_ref[...]).astype(BF16)
    cols = []
    for g in range(GMLP_GROUPS):
        rhs = jnp.concatenate(
            [vn[c * GMLP_CHUNK:(c + 1) * GMLP_CHUNK, g * gd:(g + 1) * gd] for c in range(nch)], axis=1)
        mixed = _dot(ws_ref[g], rhs) + jnp.concatenate([bs_ref[g]] * nch, axis=1)
        cols.append(jnp.concatenate(
            [mixed[:, c * gd:(c + 1) * gd] for c in range(nch)], axis=0))
    gm = (gu * jnp.concatenate(cols, axis=1)).astype(BF16)

    outs = []
    for hh in range(MEM_HEADS):
        sl = slice(hh * MEM_HEAD_DIM, (hh + 1) * MEM_HEAD_DIM)
        s = lax.dot_general(qm[:, sl], km_ref[:, sl], (((1,), (1,)), ((), ())),
                            preferred_element_type=F32) * (MEM_HEAD_DIM ** -0.5)
        m = jnp.max(s, axis=-1, keepdims=True)
        p = jnp.exp(s - m)
        l = jnp.sum(p, axis=-1, keepdims=True)
        outs.append(_dot(p.astype(BF16), vm_ref[:, sl]) / l)
    mo = jnp.concatenate(outs, axis=1).astype(BF16)

    att = jnp.concatenate([a_ref[j] for j in range(N_SLABS)], axis=1)
    gates = jax.nn.sigmoid(_dot(hb, wbg_ref[...]) + bbg_ref[...])
    merged = (gates[:, :D_MODEL] * _dot(att, wpa_ref[...])
              + gates[:, D_MODEL:2 * D_MODEL] * _dot(gm, wpg_ref[...])
              + gates[:, 2 * D_MODEL:] * _dot(mo, wpm_ref[...]))
    out = _dot(merged.astype(BF16), wo_ref[...])
    o_ref[...] = x + _rms(out, post_ref[...])


def _mixer(x3, a4, pre, w_in_b, w_bg, b_bg, gvn, ws, bs, km, vm, wpa, wpg, wpm, wo, post):
    b, s, _ = x3.shape
    tm = MIX_TM
    gd = GMLP_WIDTH // GMLP_GROUPS
    w_in_block = lambda j: pl.BlockSpec((D_MODEL, QKV_WIDTH), lambda bi, i: (0, j),
                                        pipeline_mode=pl.Buffered(1))
    return pl.pallas_call(
        _mixer_kernel,
        out_shape=jax.ShapeDtypeStruct((b, s, D_MODEL), F32),
        grid=(b, s // tm),
        in_specs=[
            pl.BlockSpec((None, tm, D_MODEL), lambda bi, i: (bi, i, 0)),
            pl.BlockSpec((None, N_SLABS, tm, LANES), lambda bi, i: (bi, 0, i, 0)),
            _const_spec((1, D_MODEL)),
            w_in_block(1),
            w_in_block(2),
            _const_spec((D_MODEL, 3 * D_MODEL)),
            _const_spec((1, 3 * D_MODEL)),
            _const_spec((1, GMLP_WIDTH)),
            _const_spec((GMLP_GROUPS, GMLP_CHUNK, GMLP_CHUNK)),
            _const_spec((GMLP_GROUPS, GMLP_CHUNK, gd)),
            pl.BlockSpec((None, MEM_LEN, MEM_WIDTH), lambda bi, i: (bi, 0, 0)),
            pl.BlockSpec((None, MEM_LEN, MEM_WIDTH), lambda bi, i: (bi, 0, 0)),
            _const_spec((ATTN_WIDTH, D_MODEL)),
            _const_spec((GMLP_WIDTH, D_MODEL)),
            _const_spec((MEM_WIDTH, D_MODEL)),
            _const_spec((D_MODEL, D_MODEL)),
            _const_spec((1, D_MODEL)),
        ],
        out_specs=pl.BlockSpec((None, tm, D_MODEL), lambda bi, i: (bi, i, 0)),
        compiler_params=pltpu.CompilerParams(
            dimension_semantics=("arbitrary", "arbitrary"), vmem_limit_bytes=VMEM_LIMIT),
        name="mixer",
    )(x3, a4, pre, w_in_b, w_in_b, w_bg, b_bg, gvn, ws, bs, km, vm, wpa, wpg, wpm, wo, post)


def _rope_tables(seq):
    rows = seq // GRID_W
    row = jnp.repeat(jnp.arange(rows, dtype=F32), GRID_W)
    col = jnp.tile(jnp.arange(GRID_W, dtype=F32), rows)
    inv_freq = ROPE_THETA ** (-jnp.arange(ROPE_NFREQ, dtype=F32) / ROPE_NFREQ)
    ang = jnp.stack([row[:, None] * inv_freq, col[:, None] * inv_freq], axis=1)
    cos, sin = jnp.cos(ang), jnp.sin(ang)
    cos64 = jnp.concatenate([cos[:, 0], cos[:, 0], cos[:, 1], cos[:, 1]], axis=1)
    sin64 = jnp.concatenate([-sin[:, 0], sin[:, 0], -sin[:, 1], sin[:, 1]], axis=1)
    reps = LANES // HEAD_DIM
    return jnp.tile(cos64, (1, reps)), jnp.tile(sin64, (1, reps))


def _block_ones(width):
    idx = np.arange(width) // HEAD_DIM
    return jnp.asarray((idx[:, None] == idx[None, :]).astype(np.float32), dtype=BF16)


def kernel(x, mem, ffn1_pre, ffn1_w_gate, ffn1_w_up, ffn1_w_down, ffn1_post, mix_pre, mem_norm, w_in, w_mem_kv, q_norm, k_norm, gmlp_v_norm, gmlp_w_s, gmlp_b_s, w_branch_gate, b_branch_gate, w_proj_attn, w_proj_gmlp, w_proj_mem, w_out, mix_post, ffn2_pre, ffn2_w_gate, ffn2_w_up, ffn2_w_down, ffn2_post):
    b, s, d = x.shape
    depth = w_in.shape[0]
    cos_t, sin_t = _rope_tables(s)
    ones_q = _block_ones(ATTN_WIDTH)
    ones_k = _block_ones(KV_WIDTH)
    gd = GMLP_WIDTH // GMLP_GROUPS
    row = lambda v: v.reshape(1, -1)
    bf = lambda w: w.astype(BF16)

    for l in range(depth):
        x = _ffn(x.reshape(b * s, d), row(ffn1_pre[l]), bf(ffn1_w_gate[l]), bf(ffn1_w_up[l]),
                 bf(ffn1_w_down[l]), row(ffn1_post[l])).reshape(b, s, d)

        qg = row(jnp.tile(q_norm[l], N_Q_HEADS)) * (HEAD_DIM ** -0.5 * np.log2(np.e))
        kg = row(jnp.tile(k_norm[l], N_KV_HEADS))
        q4, kt, v = _qkv(x, row(mix_pre[l]), bf(w_in[l][:, :QKV_WIDTH]), qg, kg, ones_q, ones_k,
                         cos_t, sin_t)
        km, vm = _memkv(mem, row(mem_norm[l]), bf(w_mem_kv[l]))
        a4, (w_in_b, w_bg_b, wpa_b, wpg_b, wpm_b, wo_b, wg2_b, wu2_b, wd2_b) = _attn(
            q4, kt, v, l, (w_in, w_branch_gate, w_proj_attn, w_proj_gmlp, w_proj_mem, w_out,
                           ffn2_w_gate, ffn2_w_up, ffn2_w_down))
        bs = jnp.broadcast_to(gmlp_b_s[l][:, :, None], (GMLP_GROUPS, GMLP_CHUNK, gd))
        x = _mixer(x, a4, row(mix_pre[l]), w_in_b, w_bg_b, row(b_branch_gate[l]),
                   row(gmlp_v_norm[l]), bf(gmlp_w_s[l]), bs, km, vm, wpa_b, wpg_b, wpm_b, wo_b,
                   row(mix_post[l]))

        x = _ffn(x.reshape(b * s, d), row(ffn2_pre[l]), wg2_b, wu2_b, wd2_b,
                 row(ffn2_post[l])).reshape(b, s, d)
    return x
```

```python
import functools

import numpy as np
import jax
import jax.numpy as jnp
from jax import lax
from jax.experimental import pallas as pl
from jax.experimental.pallas import tpu as pltpu

D_MODEL = 1024
MEM_LEN = 256
GRID_W = 64
EPS = 1e-6
HEAD_DIM = 64
ATTN_WIDTH = 512
N_Q_HEADS = 8
N_KV_HEADS = 2
KV_WIDTH = 128
QKV_WIDTH = ATTN_WIDTH + 2 * KV_WIDTH
ROPE_THETA = 10000.0
ROPE_NFREQ = 16
GMLP_WIDTH = 512
GMLP_GROUPS = 4
GMLP_CHUNK = 128
MEM_HEADS = 4
MEM_HEAD_DIM = 128
MEM_WIDTH = 512
D_FF = 2816

LANES = 128
BF16_SUBLANES = 16
N_SLABS = ATTN_WIDTH // LANES
VMEM_LIMIT = 56 * 1024 * 1024

FFN_TM = 1024
FFN_SUB = 512
FFN_FC = 256
FFN_STAGE_ROWS = 64
QKV_TM = 512
ATT_TQ = 512
ATT_KC = 256
MIX_TM = 512

BF16 = jnp.bfloat16
F32 = jnp.float32


def _rms(x, g):
    ms = jnp.mean(x * x, axis=-1, keepdims=True)
    return x * lax.rsqrt(ms + EPS) * g


def _dot(a, b):
    return jnp.dot(a, b, preferred_element_type=F32)


def _const_spec(shape):
    nd = len(shape)
    return pl.BlockSpec(shape, lambda *_: (0,) * nd, pipeline_mode=pl.Buffered(1))


def _cast_block_rows(n_rows, n_steps):
    for rows in range(BF16_SUBLANES, n_rows + 1, BF16_SUBLANES):
        if n_rows % rows == 0 and n_rows // rows <= n_steps:
            return rows
    raise ValueError((n_rows, n_steps))


def _cast_specs(weights, layer, n_grid, step_of):
    in_specs, out_specs, out_shapes = [], [], []
    for w in weights:
        _, n_rows, n_cols = w.shape
        rows = _cast_block_rows(n_rows, n_grid)
        last = n_rows // rows - 1

        def blk(*ids, last=last):
            return jnp.minimum(step_of(*ids), last)

        in_specs.append(pl.BlockSpec((None, rows, n_cols), lambda *ids, blk=blk: (layer, blk(*ids), 0)))
        out_specs.append(pl.BlockSpec((rows, n_cols), lambda *ids, blk=blk: (blk(*ids), 0)))
        out_shapes.append(jax.ShapeDtypeStruct((n_rows, n_cols), BF16))
    return in_specs, out_specs, out_shapes


def _stage_start(src_hbm, stg_ref, sem, sem_row, c):
    rows = stg_ref.shape[1]
    return pltpu.make_async_copy(src_hbm.at[pl.ds(c * rows, rows), :], stg_ref.at[c % 2],
                                 sem.at[sem_row, c % 2])


def _stage_weights(jobs, sem):
    for k, (src, _, stg) in enumerate(jobs):
        _stage_start(src, stg, sem, k, 0).start()
    for k, (src, dst, stg) in enumerate(jobs):
        rows = stg.shape[1]
        n = src.shape[0] // rows
        for c in range(n):
            if c + 1 < n:
                _stage_start(src, stg, sem, k, c + 1).start()
            _stage_start(src, stg, sem, k, c).wait()
            dst[c * rows:(c + 1) * rows, :] = stg[c % 2].astype(BF16)


def _ffn_kernel(own_layer, n_cast, x_ref, pre_ref, wg_in, wu_in, wd_in, post_ref, *refs):
    o_ref = refs[n_cast]
    if own_layer is None:
        wg_ref, wu_ref, wd_ref = wg_in, wu_in, wd_in
    else:
        wg_ref, wu_ref, wd_ref, stg_a, stg_b, stg_d, sem = refs[2 * n_cast + 1:]

        @pl.when(pl.program_id(0) == 0)
        def _():
            _stage_weights([(wg_in.at[own_layer], wg_ref, stg_a),
                            (wu_in.at[own_layer], wu_ref, stg_b),
                            (wd_in.at[own_layer], wd_ref, stg_d)], sem)

    for w_ref, wb_ref in zip(refs[:n_cast], refs[n_cast + 1:2 * n_cast + 1]):
        wb_ref[...] = w_ref[...].astype(BF16)

    n_sub = x_ref.shape[0] // FFN_SUB
    rows = [slice(r * FFN_SUB, (r + 1) * FFN_SUB) for r in range(n_sub)]
    hs = [_rms(x_ref[r, :], pre_ref[...]).astype(BF16) for r in rows]
    accs = [None] * n_sub
    for c in range(D_FF // FFN_FC):
        sl = slice(c * FFN_FC, (c + 1) * FFN_FC)
        for k in range(n_sub):
            g = _dot(hs[k], wg_ref[:, sl])
            u = _dot(hs[k], wu_ref[:, sl])
            a = (g * jax.nn.sigmoid(g) * u).astype(BF16)
            d = _dot(a, wd_ref[sl, :])
            accs[k] = d if c == 0 else accs[k] + d
    for k, r in enumerate(rows):
        o_ref[r, :] = x_ref[r, :] + 0.5 * _rms(accs[k], post_ref[...])


def _ffn(x2d, pre, wg, wu, wd, post, own_layer=None, cast_layer=0, cast_weights=()):
    t = x2d.shape[0]
    n_grid = t // FFN_TM
    c_in, c_out, c_shapes = _cast_specs(cast_weights, cast_layer, n_grid, lambda i: i)
    if own_layer is None:
        w_specs = [_const_spec((D_MODEL, D_FF)), _const_spec((D_MODEL, D_FF)),
                   _const_spec((D_FF, D_MODEL))]
        scratch = []
    else:
        w_specs = [pl.BlockSpec(memory_space=pl.ANY)] * 3
        scratch = [
            pltpu.VMEM((D_MODEL, D_FF), BF16),
            pltpu.VMEM((D_MODEL, D_FF), BF16),
            pltpu.VMEM((D_FF, D_MODEL), BF16),
            pltpu.VMEM((2, FFN_STAGE_ROWS, D_FF), F32),
            pltpu.VMEM((2, FFN_STAGE_ROWS, D_FF), F32),
            pltpu.VMEM((2, FFN_STAGE_ROWS * D_FF // D_MODEL, D_MODEL), F32),
            pltpu.SemaphoreType.DMA((3, 2)),
        ]
    outs = pl.pallas_call(
        functools.partial(_ffn_kernel, own_layer, len(cast_weights)),
        out_shape=[jax.ShapeDtypeStruct((t, D_MODEL), F32)] + c_shapes,
        grid=(n_grid,),
        in_specs=[pl.BlockSpec((FFN_TM, D_MODEL), lambda i: (i, 0)), _const_spec((1, D_MODEL))]
        + w_specs + [_const_spec((1, D_MODEL))] + c_in,
        out_specs=[pl.BlockSpec((FFN_TM, D_MODEL), lambda i: (i, 0))] + c_out,
        scratch_shapes=scratch,
        compiler_params=pltpu.CompilerParams(
            dimension_semantics=("arbitrary",), vmem_limit_bytes=VMEM_LIMIT),
        name="ffn",
    )(x2d, pre, wg, wu, wd, post, *cast_weights)
    return outs[0], outs[1:]


def _head_norm(z, ones_bd, gain):
    z2 = z * z
    hi = z2.astype(BF16)
    lo = (z2 - hi.astype(F32)).astype(BF16)
    ssum = _dot(hi, ones_bd) + _dot(lo, ones_bd)
    return z * lax.rsqrt(ssum * (1.0 / HEAD_DIM) + EPS) * gain


def _rope(x, cos_t, sin_t, first_half):
    up = pltpu.roll(x, LANES - ROPE_NFREQ, axis=1)
    dn = pltpu.roll(x, ROPE_NFREQ, axis=1)
    return x * cos_t + jnp.where(first_half, up, dn) * sin_t


def _qkv_kernel(x_ref, pre_ref, w_ref, qg_ref, kg_ref, onesq_ref, onesk_ref,
                cos_ref, sin_ref, q_ref, kt_ref, v_ref):
    h = _rms(x_ref[...], pre_ref[...]).astype(BF16)
    z = _dot(h, w_ref[...])
    cos_t = cos_ref[...]
    sin_t = sin_ref[...]
    lane = lax.broadcasted_iota(jnp.int32, cos_t.shape, 1)
    first_half = (lane % (2 * ROPE_NFREQ)) < ROPE_NFREQ
    qn = _head_norm(z[:, :ATTN_WIDTH], onesq_ref[...], qg_ref[...])
    for j in range(N_SLABS):
        q_ref[j] = _rope(qn[:, j * LANES:(j + 1) * LANES], cos_t, sin_t, first_half).astype(BF16)
    kn = _head_norm(z[:, ATTN_WIDTH:ATTN_WIDTH + KV_WIDTH], onesk_ref[...], kg_ref[...])
    kr = _rope(kn, cos_t, sin_t, first_half)
    kt = kr.T.astype(BF16)
    kt_ref[0] = kt[:HEAD_DIM]
    kt_ref[1] = kt[HEAD_DIM:]
    v_ref[...] = z[:, ATTN_WIDTH + KV_WIDTH:].astype(BF16)


def _qkv(x3, pre, w_qkv, qg, kg, ones_q, ones_k, cos_t, sin_t):
    b, s, _ = x3.shape
    tm = QKV_TM
    return pl.pallas_call(
        _qkv_kernel,
        out_shape=(
            jax.ShapeDtypeStruct((b, N_SLABS, s, LANES), BF16),
            jax.ShapeDtypeStruct((b, N_KV_HEADS, HEAD_DIM, s), BF16),
            jax.ShapeDtypeStruct((b, s, KV_WIDTH), BF16),
        ),
        grid=(b, s // tm),
        in_specs=[
            pl.BlockSpec((None, tm, D_MODEL), lambda bi, i: (bi, i, 0)),
            _const_spec((1, D_MODEL)),
            _const_spec((D_MODEL, QKV_WIDTH)),
            _const_spec((1, ATTN_WIDTH)),
            _const_spec((1, KV_WIDTH)),
            _const_spec((ATTN_WIDTH, ATTN_WIDTH)),
            _const_spec((KV_WIDTH, KV_WIDTH)),
            pl.BlockSpec((tm, LANES), lambda bi, i: (i, 0)),
            pl.BlockSpec((tm, LANES), lambda bi, i: (i, 0)),
        ],
        out_specs=(
            pl.BlockSpec((None, N_SLABS, tm, LANES), lambda bi, i: (bi, 0, i, 0)),
            pl.BlockSpec((None, N_KV_HEADS, HEAD_DIM, tm), lambda bi, i: (bi, 0, 0, i)),
            pl.BlockSpec((None, tm, KV_WIDTH), lambda bi, i: (bi, i, 0)),
        ),
        compiler_params=pltpu.CompilerParams(
            dimension_semantics=("arbitrary", "arbitrary"), vmem_limit_bytes=VMEM_LIMIT),
        name="qkv",
    )(x3, pre, w_qkv, qg, kg, ones_q, ones_k, cos_t, sin_t)


def _memkv_kernel(m_ref, g_ref, w_ref, k_ref, v_ref):
    mn = _rms(m_ref[...], g_ref[...]).astype(BF16)
    kv = _dot(mn, w_ref[...])
    k_ref[...] = kv[:, :MEM_WIDTH].astype(BF16)
    v_ref[...] = kv[:, MEM_WIDTH:].astype(BF16)


def _memkv(mem, g, w):
    b = mem.shape[0]
    return pl.pallas_call(
        _memkv_kernel,
        out_shape=(
            jax.ShapeDtypeStruct((b, MEM_LEN, MEM_WIDTH), BF16),
            jax.ShapeDtypeStruct((b, MEM_LEN, MEM_WIDTH), BF16),
        ),
        grid=(b,),
        in_specs=[
            pl.BlockSpec((None, MEM_LEN, D_MODEL), lambda bi: (bi, 0, 0)),
            _const_spec((1, D_MODEL)),
            _const_spec((D_MODEL, 2 * MEM_WIDTH)),
        ],
        out_specs=(
            pl.BlockSpec((None, MEM_LEN, MEM_WIDTH), lambda bi: (bi, 0, 0)),
            pl.BlockSpec((None, MEM_LEN, MEM_WIDTH), lambda bi: (bi, 0, 0)),
        ),
        compiler_params=pltpu.CompilerParams(dimension_semantics=("arbitrary",)),
        name="memkv",
    )(mem, g, w)


def _attn_kernel(n_cast, q_ref, kt_ref, v_ref, *refs):
    o_ref = refs[n_cast]
    for w_ref, wb_ref in zip(refs[:n_cast], refs[n_cast + 1:]):
        wb_ref[...] = w_ref[...].astype(BF16)

    tq = q_ref.shape[1]
    seq = v_ref.shape[0]
    lane = lax.broadcasted_iota(jnp.int32, (tq, LANES), 1)
    lo_half = lane < HEAD_DIM
    ones = jnp.ones((ATT_KC, LANES), BF16)
    grp = pl.program_id(1)
    for j in range(q_ref.shape[0]):
        q = q_ref[j]
        zero = jnp.zeros_like(q)
        halves = []
        for par in range(2):
            lhs = jnp.where(lo_half, q, zero) if par == 0 else jnp.where(lo_half, zero, q)
            m = acc = None
            for c in range(seq // ATT_KC):
                ksl = slice(c * ATT_KC, (c + 1) * ATT_KC)
                kt = kt_ref[:, ksl]
                s = _dot(lhs, jnp.concatenate([kt, kt], axis=0))
                v_ext = jnp.concatenate([v_ref[ksl, :], ones], axis=1)
                mc = jnp.max(s, axis=-1, keepdims=True)
                if c == 0:
                    m = mc
                    acc = _dot(jnp.exp2(s - m).astype(BF16), v_ext)
                else:
                    m_new = jnp.maximum(m, mc)
                    acc = jnp.exp2(m - m_new) * acc + _dot(jnp.exp2(s - m_new).astype(BF16), v_ext)
                    m = m_new
            o = acc[:, :LANES] / acc[:, LANES:]
            o_sw = pltpu.roll(o, HEAD_DIM, axis=1)
            halves.append(jnp.where(grp == par, o, o_sw))
        o_ref[j] = jnp.where(lo_half, halves[0], halves[1]).astype(BF16)


def _attn(q4, kt, v, layer, weights):
    b, _, s, _ = q4.shape
    tq = ATT_TQ
    per_grp = N_SLABS // N_KV_HEADS
    n_tiles = s // tq
    w_in_specs, w_out_specs, w_out_shapes = _cast_specs(
        weights, layer, b * N_KV_HEADS * n_tiles,
        lambda bi, g, i: (bi * N_KV_HEADS + g) * n_tiles + i)
    outs = pl.pallas_call(
        functools.partial(_attn_kernel, len(weights)),
        out_shape=[jax.ShapeDtypeStruct((b, N_SLABS, s, LANES), BF16)] + w_out_shapes,
        grid=(b, N_KV_HEADS, n_tiles),
        in_specs=[
            pl.BlockSpec((None, per_grp, tq, LANES), lambda bi, g, i: (bi, g, i, 0)),
            pl.BlockSpec((None, None, HEAD_DIM, s), lambda bi, g, i: (bi, g, 0, 0)),
            pl.BlockSpec((None, s, KV_WIDTH), lambda bi, g, i: (bi, 0, 0)),
        ] + w_in_specs,
        out_specs=[pl.BlockSpec((None, per_grp, tq, LANES), lambda bi, g, i: (bi, g, i, 0))]
        + w_out_specs,
        compiler_params=pltpu.CompilerParams(
            dimension_semantics=("arbitrary", "arbitrary", "arbitrary"),
            vmem_limit_bytes=VMEM_LIMIT),
        name="attn",
    )(q4, kt, v, *weights)
    return outs[0], outs[1:]


def _mixer_kernel(x_ref, a_ref, pre_ref, wra_ref, wrb_ref, wbg_ref, bbg_ref, gvn_ref, ws_ref,
                  bs_ref, km_ref, vm_ref, wpa_ref, wpg_ref, wpm_ref, wo_ref, post_ref, o_ref):
    x = x_ref[...]
    tm = x.shape[0]
    nch = tm // GMLP_CHUNK
    gd = GMLP_WIDTH // GMLP_GROUPS
    hb = _rms(x, pre_ref[...]).astype(BF16)
    z = jnp.concatenate([_dot(hb, wra_ref[...]), _dot(hb, wrb_ref[...])], axis=1)
    gu = jax.nn.gelu(z[:, :GMLP_WIDTH])
    gv = jax.nn.gelu(z[:, GMLP_WIDTH:2 * GMLP_WIDTH])
    qm = z[:, 2 * GMLP_WIDTH:].astype(BF16)

    vn = _rms(gv, gvn_ref[...]).astype(BF16)
    cols = []
    for g in range(GMLP_GROUPS):
        rhs = jnp.concatenate(
            [vn[c * GMLP_CHUNK:(c + 1) * GMLP_CHUNK, g * gd:(g + 1) * gd] for c in range(nch)], axis=1)
        mixed = _dot(ws_ref[g], rhs) + jnp.concatenate([bs_ref[g]] * nch, axis=1)
        cols.append(jnp.concatenate(
            [mixed[:, c * gd:(c + 1) * gd] for c in range(nch)], axis=0))
    gm = (gu * jnp.concatenate(cols, axis=1)).astype(BF16)

    outs = []
    for hh in range(MEM_HEADS):
        sl = slice(hh * MEM_HEAD_DIM, (hh + 1) * MEM_HEAD_DIM)
        s = lax.dot_general(qm[:, sl], km_ref[:, sl], (((1,), (1,)), ((), ())),
                            preferred_element_type=F32) * (MEM_HEAD_DIM ** -0.5)
        m = jnp.max(s, axis=-1, keepdims=True)
        p = jnp.exp(s - m)
        l = jnp.sum(p, axis=-1, keepdims=True)
        outs.append(_dot(p.astype(BF16), vm_ref[:, sl]) / l)
    mo = jnp.concatenate(outs, axis=1).astype(BF16)

    att = jnp.concatenate([a_ref[j] for j in range(N_SLABS)], axis=1)
    gates = jax.nn.sigmoid(_dot(hb, wbg_ref[...]) + bbg_ref[...])
    merged = (gates[:, :D_MODEL] * _dot(att, wpa_ref[...])
              + gates[:, D_MODEL:2 * D_MODEL] * _dot(gm, wpg_ref[...])
              + gates[:, 2 * D_MODEL:] * _dot(mo, wpm_ref[...]))
    out = _dot(merged.astype(BF16), wo_ref[...])
    o_ref[...] = x + _rms(out, post_ref[...])


def _mixer(x3, a4, pre, w_in_b, w_bg, b_bg, gvn, ws, bs, km, vm, wpa, wpg, wpm, wo, post):
    b, s, _ = x3.shape
    tm = MIX_TM
    gd = GMLP_WIDTH // GMLP_GROUPS
    w_in_block = lambda j: pl.BlockSpec((D_MODEL, QKV_WIDTH), lambda bi, i: (0, j),
                                        pipeline_mode=pl.Buffered(1))
    return pl.pallas_call(
        _mixer_kernel,
        out_shape=jax.ShapeDtypeStruct((b, s, D_MODEL), F32),
        grid=(b, s // tm),
        in_specs=[
            pl.BlockSpec((None, tm, D_MODEL), lambda bi, i: (bi, i, 0)),
            pl.BlockSpec((None, N_SLABS, tm, LANES), lambda bi, i: (bi, 0, i, 0)),
            _const_spec((1, D_MODEL)),
            w_in_block(1),
            w_in_block(2),
            _const_spec((D_MODEL, 3 * D_MODEL)),
            _const_spec((1, 3 * D_MODEL)),
            _const_spec((1, GMLP_WIDTH)),
            _const_spec((GMLP_GROUPS, GMLP_CHUNK, GMLP_CHUNK)),
            _const_spec((GMLP_GROUPS, GMLP_CHUNK, gd)),
            pl.BlockSpec((None, MEM_LEN, MEM_WIDTH), lambda bi, i: (bi, 0, 0)),
            pl.BlockSpec((None, MEM_LEN, MEM_WIDTH), lambda bi, i: (bi, 0, 0)),
            _const_spec((ATTN_WIDTH, D_MODEL)),
            _const_spec((GMLP_WIDTH, D_MODEL)),
            _const_spec((MEM_WIDTH, D_MODEL)),
            _const_spec((D_MODEL, D_MODEL)),
            _const_spec((1, D_MODEL)),
        ],
        out_specs=pl.BlockSpec((None, tm, D_MODEL), lambda bi, i: (bi, i, 0)),
        compiler_params=pltpu.CompilerParams(
            dimension_semantics=("arbitrary", "arbitrary"), vmem_limit_bytes=VMEM_LIMIT),
        name="mixer",
    )(x3, a4, pre, w_in_b, w_in_b, w_bg, b_bg, gvn, ws, bs, km, vm, wpa, wpg, wpm, wo, post)


def _rope_tables(seq):
    rows = seq // GRID_W
    row = jnp.repeat(jnp.arange(rows, dtype=F32), GRID_W)
    col = jnp.tile(jnp.arange(GRID_W, dtype=F32), rows)
    inv_freq = ROPE_THETA ** (-jnp.arange(ROPE_NFREQ, dtype=F32) / ROPE_NFREQ)
    ang = jnp.stack([row[:, None] * inv_freq, col[:, None] * inv_freq], axis=1)
    cos, sin = jnp.cos(ang), jnp.sin(ang)
    cos64 = jnp.concatenate([cos[:, 0], cos[:, 0], cos[:, 1], cos[:, 1]], axis=1)
    sin64 = jnp.concatenate([-sin[:, 0], sin[:, 0], -sin[:, 1], sin[:, 1]], axis=1)
    reps = LANES // HEAD_DIM
    return jnp.tile(cos64, (1, reps)), jnp.tile(sin64, (1, reps))


def _block_ones(width):
    idx = np.arange(width) // HEAD_DIM
    return jnp.asarray((idx[:, None] == idx[None, :]).astype(np.float32), dtype=BF16)


def kernel(x, mem, ffn1_pre, ffn1_w_gate, ffn1_w_up, ffn1_w_down, ffn1_post, mix_pre, mem_norm, w_in, w_mem_kv, q_norm, k_norm, gmlp_v_norm, gmlp_w_s, gmlp_b_s, w_branch_gate, b_branch_gate, w_proj_attn, w_proj_gmlp, w_proj_mem, w_out, mix_post, ffn2_pre, ffn2_w_gate, ffn2_w_up, ffn2_w_down, ffn2_post):
    b, s, d = x.shape
    depth = w_in.shape[0]
    cos_t, sin_t = _rope_tables(s)
    ones_q = _block_ones(ATTN_WIDTH)
    ones_k = _block_ones(KV_WIDTH)
    gd = GMLP_WIDTH // GMLP_GROUPS
    row = lambda v: v.reshape(1, -1)
    bf = lambda w: w.astype(BF16)

    for l in range(depth):
        x, (w_in_b, w_mkv_b, w_bg_b, wpa_b, wpg_b, wpm_b, wo_b) = _ffn(
            x.reshape(b * s, d), row(ffn1_pre[l]), ffn1_w_gate, ffn1_w_up, ffn1_w_down,
            row(ffn1_post[l]), own_layer=l, cast_layer=l,
            cast_weights=(w_in, w_mem_kv, w_branch_gate, w_proj_attn, w_proj_gmlp, w_proj_mem,
                          w_out))
        x = x.reshape(b, s, d)

        qg = row(jnp.tile(q_norm[l], N_Q_HEADS)) * (HEAD_DIM ** -0.5 * np.log2(np.e))
        kg = row(jnp.tile(k_norm[l], N_KV_HEADS))
        q4, kt, v = _qkv(x, row(mix_pre[l]), w_in_b, qg, kg, ones_q, ones_k, cos_t, sin_t)
        km, vm = _memkv(mem, row(mem_norm[l]), w_mkv_b)
        a4, (wg2_b, wu2_b, wd2_b) = _attn(q4, kt, v, l, (ffn2_w_gate, ffn2_w_up, ffn2_w_down))
        bs = jnp.broadcast_to(gmlp_b_s[l][:, :, None], (GMLP_GROUPS, GMLP_CHUNK, gd))
        x = _mixer(x, a4, row(mix_pre[l]), w_in_b, w_bg_b, row(b_branch_gate[l]),
                   row(gmlp_v_norm[l]), bf(gmlp_w_s[l]), bs, km, vm, wpa_b, wpg_b, wpm_b, wo_b,
                   row(mix_post[l]))

        x, _ = _ffn(x.reshape(b * s, d), row(ffn2_pre[l]), wg2_b, wu2_b, wd2_b, row(ffn2_post[l]))
        x = x.reshape(b, s, d)
    return x
```

```python
import functools

import numpy as np
import jax
import jax.numpy as jnp
from jax import lax
from jax.experimental import pallas as pl
from jax.experimental.pallas import tpu as pltpu

D_MODEL = 1024
MEM_LEN = 256
GRID_W = 64
EPS = 1e-6
HEAD_DIM = 64
ATTN_WIDTH = 512
N_Q_HEADS = 8
N_KV_HEADS = 2
KV_WIDTH = 128
QKV_WIDTH = ATTN_WIDTH + 2 * KV_WIDTH
ROPE_THETA = 10000.0
ROPE_NFREQ = 16
GMLP_WIDTH = 512
GMLP_GROUPS = 4
GMLP_CHUNK = 128
MEM_HEADS = 4
MEM_HEAD_DIM = 128
MEM_WIDTH = 512
D_FF = 2816

LANES = 128
BF16_SUBLANES = 16
N_SLABS = ATTN_WIDTH // LANES
VMEM_LIMIT = 56 * 1024 * 1024

FFN_TM = 1024
FFN_SUB = 512
FFN_FC = 256
FFN_STAGE_ROWS = 64
FFN_STAGE_SLOTS = 3
QKV_TM = 512
ATT_TQ = 512
ATT_KC = 256
MIX_TM = 512

BF16 = jnp.bfloat16
F32 = jnp.float32


def _rms(x, g):
    ms = jnp.mean(x * x, axis=-1, keepdims=True)
    return x * lax.rsqrt(ms + EPS) * g


def _dot(a, b):
    return jnp.dot(a, b, preferred_element_type=F32)


def _const_spec(shape):
    nd = len(shape)
    return pl.BlockSpec(shape, lambda *_: (0,) * nd, pipeline_mode=pl.Buffered(1))


def _cast_block_rows(n_rows, n_steps):
    for rows in range(BF16_SUBLANES, n_rows + 1, BF16_SUBLANES):
        if n_rows % rows == 0 and n_rows // rows <= n_steps:
            return rows
    raise ValueError((n_rows, n_steps))


def _cast_specs(weights, layer, n_grid, step_of):
    in_specs, out_specs, out_shapes = [], [], []
    for w in weights:
        _, n_rows, n_cols = w.shape
        rows = _cast_block_rows(n_rows, n_grid)
        last = n_rows // rows - 1

        def blk(*ids, last=last):
            return jnp.minimum(step_of(*ids), last)

        in_specs.append(pl.BlockSpec((None, rows, n_cols), lambda *ids, blk=blk: (layer, blk(*ids), 0)))
        out_specs.append(pl.BlockSpec((rows, n_cols), lambda *ids, blk=blk: (blk(*ids), 0)))
        out_shapes.append(jax.ShapeDtypeStruct((n_rows, n_cols), BF16))
    return in_specs, out_specs, out_shapes


def _stage_copy(src_hbm, stg_ref, sem, job, c):
    slots, rows = stg_ref.shape[0], stg_ref.shape[1]
    return pltpu.make_async_copy(src_hbm.at[pl.ds(c * rows, rows), :], stg_ref.at[c % slots],
                                 sem.at[job, c % slots])


def _stage_weights(jobs, sem):
    n_chunks = [src.shape[0] // stg.shape[1] for src, _, stg in jobs]
    for k, (src, _, stg) in enumerate(jobs):
        for c in range(min(stg.shape[0], n_chunks[k])):
            _stage_copy(src, stg, sem, k, c).start()
    for c in range(max(n_chunks)):
        for k, (src, dst, stg) in enumerate(jobs):
            if c >= n_chunks[k]:
                continue
            slots, rows = stg.shape[0], stg.shape[1]
            _stage_copy(src, stg, sem, k, c).wait()
            dst[c * rows:(c + 1) * rows, :] = stg[c % slots].astype(BF16)
            if c + slots < n_chunks[k]:
                _stage_copy(src, stg, sem, k, c + slots).start()


def _ffn_kernel(own_layer, n_cast, x_ref, pre_ref, wg_in, wu_in, wd_in, post_ref, *refs):
    o_ref = refs[n_cast]
    if own_layer is None:
        wg_ref, wu_ref, wd_ref = wg_in, wu_in, wd_in
    else:
        wg_ref, wu_ref, wd_ref, stg_a, stg_b, stg_d, sem = refs[2 * n_cast + 1:]

        @pl.when(pl.program_id(0) == 0)
        def _():
            _stage_weights([(wg_in.at[own_layer], wg_ref, stg_a),
                            (wu_in.at[own_layer], wu_ref, stg_b),
                            (wd_in.at[own_layer], wd_ref, stg_d)], sem)

    for w_ref, wb_ref in zip(refs[:n_cast], refs[n_cast + 1:2 * n_cast + 1]):
        wb_ref[...] = w_ref[...].astype(BF16)

    n_sub = x_ref.shape[0] // FFN_SUB
    rows = [slice(r * FFN_SUB, (r + 1) * FFN_SUB) for r in range(n_sub)]
    hs = [_rms(x_ref[r, :], pre_ref[...]).astype(BF16) for r in rows]
    accs = [None] * n_sub
    for c in range(D_FF // FFN_FC):
        sl = slice(c * FFN_FC, (c + 1) * FFN_FC)
        for k in range(n_sub):
            g = _dot(hs[k], wg_ref[:, sl])
            u = _dot(hs[k], wu_ref[:, sl])
            a = (g * jax.nn.sigmoid(g) * u).astype(BF16)
            d = _dot(a, wd_ref[sl, :])
            accs[k] = d if c == 0 else accs[k] + d
    for k, r in enumerate(rows):
        o_ref[r, :] = x_ref[r, :] + 0.5 * _rms(accs[k], post_ref[...])


def _ffn(x2d, pre, wg, wu, wd, post, own_layer=None, cast_layer=0, cast_weights=()):
    t = x2d.shape[0]
    n_grid = t // FFN_TM
    c_in, c_out, c_shapes = _cast_specs(cast_weights, cast_layer, n_grid, lambda i: i)
    if own_layer is None:
        w_specs = [_const_spec((D_MODEL, D_FF)), _const_spec((D_MODEL, D_FF)),
                   _const_spec((D_FF, D_MODEL))]
        scratch = []
    else:
        w_specs = [pl.BlockSpec(memory_space=pl.ANY)] * 3
        scratch = [
            pltpu.VMEM((D_MODEL, D_FF), BF16),
            pltpu.VMEM((D_MODEL, D_FF), BF16),
            pltpu.VMEM((D_FF, D_MODEL), BF16),
            pltpu.VMEM((FFN_STAGE_SLOTS, FFN_STAGE_ROWS, D_FF), F32),
            pltpu.VMEM((FFN_STAGE_SLOTS, FFN_STAGE_ROWS, D_FF), F32),
            pltpu.VMEM((FFN_STAGE_SLOTS, FFN_STAGE_ROWS * D_FF // D_MODEL, D_MODEL), F32),
            pltpu.SemaphoreType.DMA((3, FFN_STAGE_SLOTS)),
        ]
    outs = pl.pallas_call(
        functools.partial(_ffn_kernel, own_layer, len(cast_weights)),
        out_shape=[jax.ShapeDtypeStruct((t, D_MODEL), F32)] + c_shapes,
        grid=(n_grid,),
        in_specs=[pl.BlockSpec((FFN_TM, D_MODEL), lambda i: (i, 0)), _const_spec((1, D_MODEL))]
        + w_specs + [_const_spec((1, D_MODEL))] + c_in,
        out_specs=[pl.BlockSpec((FFN_TM, D_MODEL), lambda i: (i, 0))] + c_out,
        scratch_shapes=scratch,
        compiler_params=pltpu.CompilerParams(
            dimension_semantics=("arbitrary",), vmem_limit_bytes=VMEM_LIMIT),
        name="ffn",
    )(x2d, pre, wg, wu, wd, post, *cast_weights)
    return outs[0], outs[1:]


def _head_norm(z, ones_bd, gain):
    z2 = z * z
    hi = z2.astype(BF16)
    lo = (z2 - hi.astype(F32)).astype(BF16)
    ssum = _dot(hi, ones_bd) + _dot(lo, ones_bd)
    return z * lax.rsqrt(ssum * (1.0 / HEAD_DIM) + EPS) * gain


def _rope(x, cos_t, sin_t, first_half):
    up = pltpu.roll(x, LANES - ROPE_NFREQ, axis=1)
    dn = pltpu.roll(x, ROPE_NFREQ, axis=1)
    return x * cos_t + jnp.where(first_half, up, dn) * sin_t


def _qkv_kernel(x_ref, pre_ref, w_ref, qg_ref, kg_ref, onesq_ref, onesk_ref,
                cos_ref, sin_ref, q_ref, kt_ref, v_ref):
    h = _rms(x_ref[...], pre_ref[...]).astype(BF16)
    z = _dot(h, w_ref[...])
    cos_t = cos_ref[...]
    sin_t = sin_ref[...]
    lane = lax.broadcasted_iota(jnp.int32, cos_t.shape, 1)
    first_half = (lane % (2 * ROPE_NFREQ)) < ROPE_NFREQ
    qn = _head_norm(z[:, :ATTN_WIDTH], onesq_ref[...], qg_ref[...])
    for j in range(N_SLABS):
        q_ref[j] = _rope(qn[:, j * LANES:(j + 1) * LANES], cos_t, sin_t, first_half).astype(BF16)
    kn = _head_norm(z[:, ATTN_WIDTH:ATTN_WIDTH + KV_WIDTH], onesk_ref[...], kg_ref[...])
    kr = _rope(kn, cos_t, sin_t, first_half)
    kt = kr.T.astype(BF16)
    kt_ref[0] = kt[:HEAD_DIM]
    kt_ref[1] = kt[HEAD_DIM:]
    v_ref[...] = z[:, ATTN_WIDTH + KV_WIDTH:].astype(BF16)


def _qkv(x3, pre, w_qkv, qg, kg, ones_q, ones_k, cos_t, sin_t):
    b, s, _ = x3.shape
    tm = QKV_TM
    return pl.pallas_call(
        _qkv_kernel,
        out_shape=(
            jax.ShapeDtypeStruct((b, N_SLABS, s, LANES), BF16),
            jax.ShapeDtypeStruct((b, N_KV_HEADS, HEAD_DIM, s), BF16),
            jax.ShapeDtypeStruct((b, s, KV_WIDTH), BF16),
        ),
        grid=(b, s // tm),
        in_specs=[
            pl.BlockSpec((None, tm, D_MODEL), lambda bi, i: (bi, i, 0)),
            _const_spec((1, D_MODEL)),
            _const_spec((D_MODEL, QKV_WIDTH)),
            _const_spec((1, ATTN_WIDTH)),
            _const_spec((1, KV_WIDTH)),
            _const_spec((ATTN_WIDTH, ATTN_WIDTH)),
            _const_spec((KV_WIDTH, KV_WIDTH)),
            pl.BlockSpec((tm, LANES), lambda bi, i: (i, 0)),
            pl.BlockSpec((tm, LANES), lambda bi, i: (i, 0)),
        ],
        out_specs=(
            pl.BlockSpec((None, N_SLABS, tm, LANES), lambda bi, i: (bi, 0, i, 0)),
            pl.BlockSpec((None, N_KV_HEADS, HEAD_DIM, tm), lambda bi, i: (bi, 0, 0, i)),
            pl.BlockSpec((None, tm, KV_WIDTH), lambda bi, i: (bi, i, 0)),
        ),
        compiler_params=pltpu.CompilerParams(
            dimension_semantics=("arbitrary", "arbitrary"), vmem_limit_bytes=VMEM_LIMIT),
        name="qkv",
    )(x3, pre, w_qkv, qg, kg, ones_q, ones_k, cos_t, sin_t)


def _memkv_kernel(m_ref, g_ref, w_ref, k_ref, v_ref):
    mn = _rms(m_ref[...], g_ref[...]).astype(BF16)
    kv = _dot(mn, w_ref[...])
    k_ref[...] = kv[:, :MEM_WIDTH].astype(BF16)
    v_ref[...] = kv[:, MEM_WIDTH:].astype(BF16)


def _memkv(mem, g, w):
    b = mem.shape[0]
    return pl.pallas_call(
        _memkv_kernel,
        out_shape=(
            jax.ShapeDtypeStruct((b, MEM_LEN, MEM_WIDTH), BF16),
            jax.ShapeDtypeStruct((b, MEM_LEN, MEM_WIDTH), BF16),
        ),
        grid=(b,),
        in_specs=[
            pl.BlockSpec((None, MEM_LEN, D_MODEL), lambda bi: (bi, 0, 0)),
            _const_spec((1, D_MODEL)),
            _const_spec((D_MODEL, 2 * MEM_WIDTH)),
        ],
        out_specs=(
            pl.BlockSpec((None, MEM_LEN, MEM_WIDTH), lambda bi: (bi, 0, 0)),
            pl.BlockSpec((None, MEM_LEN, MEM_WIDTH), lambda bi: (bi, 0, 0)),
        ),
        compiler_params=pltpu.CompilerParams(dimension_semantics=("arbitrary",)),
        name="memkv",
    )(mem, g, w)


def _attn_kernel(n_cast, q_ref, kt_ref, v_ref, *refs):
    o_ref = refs[n_cast]
    for w_ref, wb_ref in zip(refs[:n_cast], refs[n_cast + 1:]):
        wb_ref[...] = w_ref[...].astype(BF16)

    tq = q_ref.shape[1]
    seq = v_ref.shape[0]
    lane = lax.broadcasted_iota(jnp.int32, (tq, LANES), 1)
    lo_half = lane < HEAD_DIM
    ones = jnp.ones((ATT_KC, LANES), BF16)
    grp = pl.program_id(1)
    for j in range(q_ref.shape[0]):
        q = q_ref[j]
        zero = jnp.zeros_like(q)
        halves = []
        for par in range(2):
            lhs = jnp.where(lo_half, q, zero) if par == 0 else jnp.where(lo_half, zero, q)
            m = acc = None
            for c in range(seq // ATT_KC):
                ksl = slice(c * ATT_KC, (c + 1) * ATT_KC)
                kt = kt_ref[:, ksl]
                s = _dot(lhs, jnp.concatenate([kt, kt], axis=0))
                v_ext = jnp.concatenate([v_ref[ksl, :], ones], axis=1)
                mc = jnp.max(s, axis=-1, keepdims=True)
                if c == 0:
                    m = mc
                    acc = _dot(jnp.exp2(s - m).astype(BF16), v_ext)
                else:
                    m_new = jnp.maximum(m, mc)
                    acc = jnp.exp2(m - m_new) * acc + _dot(jnp.exp2(s - m_new).astype(BF16), v_ext)
                    m = m_new
            o = acc[:, :LANES] / acc[:, LANES:]
            o_sw = pltpu.roll(o, HEAD_DIM, axis=1)
            halves.append(jnp.where(grp == par, o, o_sw))
        o_ref[j] = jnp.where(lo_half, halves[0], halves[1]).astype(BF16)


def _attn(q4, kt, v, layer, weights):
    b, _, s, _ = q4.shape
    tq = ATT_TQ
    per_grp = N_SLABS // N_KV_HEADS
    n_tiles = s // tq
    w_in_specs, w_out_specs, w_out_shapes = _cast_specs(
        weights, layer, b * N_KV_HEADS * n_tiles,
        lambda bi, g, i: (bi * N_KV_HEADS + g) * n_tiles + i)
    outs = pl.pallas_call(
        functools.partial(_attn_kernel, len(weights)),
        out_shape=[jax.ShapeDtypeStruct((b, N_SLABS, s, LANES), BF16)] + w_out_shapes,
        grid=(b, N_KV_HEADS, n_tiles),
        in_specs=[
            pl.BlockSpec((None, per_grp, tq, LANES), lambda bi, g, i: (bi, g, i, 0)),
            pl.BlockSpec((None, None, HEAD_DIM, s), lambda bi, g, i: (bi, g, 0, 0)),
            pl.BlockSpec((None, s, KV_WIDTH), lambda bi, g, i: (bi, 0, 0)),
        ] + w_in_specs,
        out_specs=[pl.BlockSpec((None, per_grp, tq, LANES), lambda bi, g, i: (bi, g, i, 0))]
        + w_out_specs,
        compiler_params=pltpu.CompilerParams(
            dimension_semantics=("arbitrary", "arbitrary", "arbitrary"),
            vmem_limit_bytes=VMEM_LIMIT),
        name="attn",
    )(q4, kt, v, *weights)
    return outs[0], outs[1:]


def _mixer_kernel(x_ref, a_ref, pre_ref, wra_ref, wrb_ref, wbg_ref, bbg_ref, gvn_ref, ws_ref,
                  bs_ref, km_ref, vm_ref, wpa_ref, wpg_ref, wpm_ref, wo_ref, post_ref, o_ref):
    x = x_ref[...]
    tm = x.shape[0]
    nch = tm // GMLP_CHUNK
    gd = GMLP_WIDTH // GMLP_GROUPS
    hb = _rms(x, pre_ref[...]).astype(BF16)
    z = jnp.concatenate([_dot(hb, wra_ref[...]), _dot(hb, wrb_ref[...])], axis=1)
    gu = jax.nn.gelu(z[:, :GMLP_WIDTH])
    gv = jax.nn.gelu(z[:, GMLP_WIDTH:2 * GMLP_WIDTH])
    qm = z[:, 2 * GMLP_WIDTH:].astype(BF16)

    vn = _rms(gv, gvn_ref[...]).astype(BF16)
    cols = []
    for g in range(GMLP_GROUPS):
        rhs = jnp.concatenate(
            [vn[c * GMLP_CHUNK:(c + 1) * GMLP_CHUNK, g * gd:(g + 1) * gd] for c in range(nch)], axis=1)
        mixed = _dot(ws_ref[g], rhs) + jnp.concatenate([bs_ref[g]] * nch, axis=1)
        cols.append(jnp.concatenate(
            [mixed[:, c * gd:(c + 1) * gd] for c in range(nch)], axis=0))
    gm = (gu * jnp.concatenate(cols, axis=1)).astype(BF16)

    outs = []
    for hh in range(MEM_HEADS):
        sl = slice(hh * MEM_HEAD_DIM, (hh + 1) * MEM_HEAD_DIM)
        s = lax.dot_general(qm[:, sl], km_ref[:, sl], (((1,), (1,)), ((), ())),
                            preferred_element_type=F32) * (MEM_HEAD_DIM ** -0.5)
        m = jnp.max(s, axis=-1, keepdims=True)
        p = jnp.exp(s - m)
        l = jnp.sum(p, axis=-1, keepdims=True)
        outs.append(_dot(p.astype(BF16), vm_ref[:, sl]) / l)
    mo = jnp.concatenate(outs, axis=1).astype(BF16)

    att = jnp.concatenate([a_ref[j] for j in range(N_SLABS)], axis=1)
    gates = jax.nn.sigmoid(_dot(hb, wbg_ref[...]) + bbg_ref[...])
    merged = (gates[:, :D_MODEL] * _dot(att, wpa_ref[...])
              + gates[:, D_MODEL:2 * D_MODEL] * _dot(gm, wpg_ref[...])
              + gates[:, 2 * D_MODEL:] * _dot(mo, wpm_ref[...]))
    out = _dot(merged.astype(BF16), wo_ref[...])
    o_ref[...] = x + _rms(out, post_ref[...])


def _mixer(x3, a4, pre, w_in_b, w_bg, b_bg, gvn, ws, bs, km, vm, wpa, wpg, wpm, wo, post):
    b, s, _ = x3.shape
    tm = MIX_TM
    gd = GMLP_WIDTH // GMLP_GROUPS
    w_in_block = lambda j: pl.BlockSpec((D_MODEL, QKV_WIDTH), lambda bi, i: (0, j),
                                        pipeline_mode=pl.Buffered(1))
    return pl.pallas_call(
        _mixer_kernel,
        out_shape=jax.ShapeDtypeStruct((b, s, D_MODEL), F32),
        grid=(b, s // tm),
        in_specs=[
            pl.BlockSpec((None, tm, D_MODEL), lambda bi, i: (bi, i, 0)),
            pl.BlockSpec((None, N_SLABS, tm, LANES), lambda bi, i: (bi, 0, i, 0)),
            _const_spec((1, D_MODEL)),
            w_in_block(1),
            w_in_block(2),
            _const_spec((D_MODEL, 3 * D_MODEL)),
            _const_spec((1, 3 * D_MODEL)),
            _const_spec((1, GMLP_WIDTH)),
            _const_spec((GMLP_GROUPS, GMLP_CHUNK, GMLP_CHUNK)),
            _const_spec((GMLP_GROUPS, GMLP_CHUNK, gd)),
            pl.BlockSpec((None, MEM_LEN, MEM_WIDTH), lambda bi, i: (bi, 0, 0)),
            pl.BlockSpec((None, MEM_LEN, MEM_WIDTH), lambda bi, i: (bi, 0, 0)),
            _const_spec((ATTN_WIDTH, D_MODEL)),
            _const_spec((GMLP_WIDTH, D_MODEL)),
            _const_spec((MEM_WIDTH, D_MODEL)),
            _const_spec((D_MODEL, D_MODEL)),
            _const_spec((1, D_MODEL)),
        ],
        out_specs=pl.BlockSpec((None, tm, D_MODEL), lambda bi, i: (bi, i, 0)),
        compiler_params=pltpu.CompilerParams(
            dimension_semantics=("arbitrary", "arbitrary"), vmem_limit_bytes=VMEM_LIMIT),
        name="mixer",
    )(x3, a4, pre, w_in_b, w_in_b, w_bg, b_bg, gvn, ws, bs, km, vm, wpa, wpg, wpm, wo, post)


def _rope_tables(seq):
    rows = seq // GRID_W
    row = jnp.repeat(jnp.arange(rows, dtype=F32), GRID_W)
    col = jnp.tile(jnp.arange(GRID_W, dtype=F32), rows)
    inv_freq = ROPE_THETA ** (-jnp.arange(ROPE_NFREQ, dtype=F32) / ROPE_NFREQ)
    ang = jnp.stack([row[:, None] * inv_freq, col[:, None] * inv_freq], axis=1)
    cos, sin = jnp.cos(ang), jnp.sin(ang)
    cos64 = jnp.concatenate([cos[:, 0], cos[:, 0], cos[:, 1], cos[:, 1]], axis=1)
    sin64 = jnp.concatenate([-sin[:, 0], sin[:, 0], -sin[:, 1], sin[:, 1]], axis=1)
    reps = LANES // HEAD_DIM
    return jnp.tile(cos64, (1, reps)), jnp.tile(sin64, (1, reps))


def _block_ones(width):
    idx = np.arange(width) // HEAD_DIM
    return jnp.asarray((idx[:, None] == idx[None, :]).astype(np.float32), dtype=BF16)


def kernel(x, mem, ffn1_pre, ffn1_w_gate, ffn1_w_up, ffn1_w_down, ffn1_post, mix_pre, mem_norm, w_in, w_mem_kv, q_norm, k_norm, gmlp_v_norm, gmlp_w_s, gmlp_b_s, w_branch_gate, b_branch_gate, w_proj_attn, w_proj_gmlp, w_proj_mem, w_out, mix_post, ffn2_pre, ffn2_w_gate, ffn2_w_up, ffn2_w_down, ffn2_post):
    b, s, d = x.shape
    depth = w_in.shape[0]
    cos_t, sin_t = _rope_tables(s)
    ones_q = _block_ones(ATTN_WIDTH)
    ones_k = _block_ones(KV_WIDTH)
    gd = GMLP_WIDTH // GMLP_GROUPS
    row = lambda v: v.reshape(1, -1)
    bf = lambda w: w.astype(BF16)

    for l in range(depth):
        x, (w_in_b, w_mkv_b, w_bg_b, wpa_b, wpg_b, wpm_b, wo_b) = _ffn(
            x.reshape(b * s, d), row(ffn1_pre[l]), ffn1_w_gate, ffn1_w_up, ffn1_w_down,
            row(ffn1_post[l]), own_layer=l, cast_layer=l,
            cast_weights=(w_in, w_mem_kv, w_branch_gate, w_proj_attn, w_proj_gmlp, w_proj_mem,
                          w_out))
        x = x.reshape(b, s, d)

        qg = row(jnp.tile(q_norm[l], N_Q_HEADS)) * (HEAD_DIM ** -0.5 * np.log2(np.e))
        kg = row(jnp.tile(k_norm[l], N_KV_HEADS))
        q4, kt, v = _qkv(x, row(mix_pre[l]), w_in_b, qg, kg, ones_q, ones_k, cos_t, sin_t)
        km, vm = _memkv(mem, row(mem_norm[l]), w_mkv_b)
        a4, (wg2_b, wu2_b, wd2_b) = _attn(q4, kt, v, l, (ffn2_w_gate, ffn2_w_up, ffn2_w_down))
        bs = jnp.broadcast_to(gmlp_b_s[l][:, :, None], (GMLP_GROUPS, GMLP_CHUNK, gd))
        x = _mixer(x, a4, row(mix_pre[l]), w_in_b, w_bg_b, row(b_branch_gate[l]),
                   row(gmlp_v_norm[l]), bf(gmlp_w_s[l]), bs, km, vm, wpa_b, wpg_b, wpm_b, wo_b,
                   row(mix_post[l]))

        x, _ = _ffn(x.reshape(b * s, d), row(ffn2_pre[l]), wg2_b, wu2_b, wd2_b, row(ffn2_post[l]))
        x = x.reshape(b, s, d)
    return x
```

```python
import functools

import numpy as np
import jax
import jax.numpy as jnp
from jax import lax
from jax.experimental import pallas as pl
from jax.experimental.pallas import tpu as pltpu

D_MODEL = 1024
MEM_LEN = 256
GRID_W = 64
EPS = 1e-6
HEAD_DIM = 64
ATTN_WIDTH = 512
N_Q_HEADS = 8
N_KV_HEADS = 2
KV_WIDTH = 128
QKV_WIDTH = ATTN_WIDTH + 2 * KV_WIDTH
ROPE_THETA = 10000.0
ROPE_NFREQ = 16
GMLP_WIDTH = 512
GMLP_GROUPS = 4
GMLP_CHUNK = 128
MEM_HEADS = 4
MEM_HEAD_DIM = 128
MEM_WIDTH = 512
D_FF = 2816

LANES = 128
BF16_SUBLANES = 16
N_SLABS = ATTN_WIDTH // LANES
VMEM_LIMIT = 56 * 1024 * 1024

FFN_TM = 1024
FFN_SUB = 512
FFN_FC = 256
FFN_STAGE_ROWS = 64
FFN_STAGE_SLOTS = 3
QKV_TM = 512
ATT_TQ = 1024
ATT_KC = 256
MIX_TM = 512

BF16 = jnp.bfloat16
F32 = jnp.float32


def _rms(x, g):
    ms = jnp.mean(x * x, axis=-1, keepdims=True)
    return x * lax.rsqrt(ms + EPS) * g


def _dot(a, b):
    return jnp.dot(a, b, preferred_element_type=F32)


def _const_spec(shape):
    nd = len(shape)
    return pl.BlockSpec(shape, lambda *_: (0,) * nd, pipeline_mode=pl.Buffered(1))


def _cast_block_rows(n_rows, n_steps):
    for rows in range(BF16_SUBLANES, n_rows + 1, BF16_SUBLANES):
        if n_rows % rows == 0 and n_rows // rows <= n_steps:
            return rows
    raise ValueError((n_rows, n_steps))


def _cast_specs(weights, layer, n_grid, step_of):
    in_specs, out_specs, out_shapes = [], [], []
    for w in weights:
        _, n_rows, n_cols = w.shape
        rows = _cast_block_rows(n_rows, n_grid)
        last = n_rows // rows - 1

        def blk(*ids, last=last):
            return jnp.minimum(step_of(*ids), last)

        in_specs.append(pl.BlockSpec((None, rows, n_cols), lambda *ids, blk=blk: (layer, blk(*ids), 0)))
        out_specs.append(pl.BlockSpec((rows, n_cols), lambda *ids, blk=blk: (blk(*ids), 0)))
        out_shapes.append(jax.ShapeDtypeStruct((n_rows, n_cols), BF16))
    return in_specs, out_specs, out_shapes


def _stage_copy(src_hbm, stg_ref, sem, job, c):
    slots, rows = stg_ref.shape[0], stg_ref.shape[1]
    return pltpu.make_async_copy(src_hbm.at[pl.ds(c * rows, rows), :], stg_ref.at[c % slots],
                                 sem.at[job, c % slots])


def _stage_weights(jobs, sem):
    n_chunks = [src.shape[0] // stg.shape[1] for src, _, stg in jobs]
    for k, (src, _, stg) in enumerate(jobs):
        for c in range(min(stg.shape[0], n_chunks[k])):
            _stage_copy(src, stg, sem, k, c).start()
    for c in range(max(n_chunks)):
        for k, (src, dst, stg) in enumerate(jobs):
            if c >= n_chunks[k]:
                continue
            slots, rows = stg.shape[0], stg.shape[1]
            _stage_copy(src, stg, sem, k, c).wait()
            dst[c * rows:(c + 1) * rows, :] = stg[c % slots].astype(BF16)
            if c + slots < n_chunks[k]:
                _stage_copy(src, stg, sem, k, c + slots).start()


def _ffn_kernel(own_layer, n_cast, x_ref, pre_ref, wg_in, wu_in, wd_in, post_ref, *refs):
    o_ref = refs[n_cast]
    if own_layer is None:
        wg_ref, wu_ref, wd_ref = wg_in, wu_in, wd_in
    else:
        wg_ref, wu_ref, wd_ref, stg_a, stg_b, stg_d, sem = refs[2 * n_cast + 1:]

        @pl.when(pl.program_id(0) == 0)
        def _():
            _stage_weights([(wg_in.at[own_layer], wg_ref, stg_a),
                            (wu_in.at[own_layer], wu_ref, stg_b),
                            (wd_in.at[own_layer], wd_ref, stg_d)], sem)

    for w_ref, wb_ref in zip(refs[:n_cast], refs[n_cast + 1:2 * n_cast + 1]):
        wb_ref[...] = w_ref[...].astype(BF16)

    n_sub = x_ref.shape[0] // FFN_SUB
    rows = [slice(r * FFN_SUB, (r + 1) * FFN_SUB) for r in range(n_sub)]
    hs = [_rms(x_ref[r, :], pre_ref[...]).astype(BF16) for r in rows]
    accs = [None] * n_sub
    for c in range(D_FF // FFN_FC):
        sl = slice(c * FFN_FC, (c + 1) * FFN_FC)
        for k in range(n_sub):
            g = _dot(hs[k], wg_ref[:, sl])
            u = _dot(hs[k], wu_ref[:, sl])
            a = (g * jax.nn.sigmoid(g) * u).astype(BF16)
            d = _dot(a, wd_ref[sl, :])
            accs[k] = d if c == 0 else accs[k] + d
    for k, r in enumerate(rows):
        o_ref[r, :] = x_ref[r, :] + 0.5 * _rms(accs[k], post_ref[...])


def _ffn(x2d, pre, wg, wu, wd, post, own_layer=None, cast_layer=0, cast_weights=()):
    t = x2d.shape[0]
    n_grid = t // FFN_TM
    c_in, c_out, c_shapes = _cast_specs(cast_weights, cast_layer, n_grid, lambda i: i)
    if own_layer is None:
        w_specs = [_const_spec((D_MODEL, D_FF)), _const_spec((D_MODEL, D_FF)),
                   _const_spec((D_FF, D_MODEL))]
        scratch = []
    else:
        w_specs = [pl.BlockSpec(memory_space=pl.ANY)] * 3
        scratch = [
            pltpu.VMEM((D_MODEL, D_FF), BF16),
            pltpu.VMEM((D_MODEL, D_FF), BF16),
            pltpu.VMEM((D_FF, D_MODEL), BF16),
            pltpu.VMEM((FFN_STAGE_SLOTS, FFN_STAGE_ROWS, D_FF), F32),
            pltpu.VMEM((FFN_STAGE_SLOTS, FFN_STAGE_ROWS, D_FF), F32),
            pltpu.VMEM((FFN_STAGE_SLOTS, FFN_STAGE_ROWS * D_FF // D_MODEL, D_MODEL), F32),
            pltpu.SemaphoreType.DMA((3, FFN_STAGE_SLOTS)),
        ]
    outs = pl.pallas_call(
        functools.partial(_ffn_kernel, own_layer, len(cast_weights)),
        out_shape=[jax.ShapeDtypeStruct((t, D_MODEL), F32)] + c_shapes,
        grid=(n_grid,),
        in_specs=[pl.BlockSpec((FFN_TM, D_MODEL), lambda i: (i, 0)), _const_spec((1, D_MODEL))]
        + w_specs + [_const_spec((1, D_MODEL))] + c_in,
        out_specs=[pl.BlockSpec((FFN_TM, D_MODEL), lambda i: (i, 0))] + c_out,
        scratch_shapes=scratch,
        compiler_params=pltpu.CompilerParams(
            dimension_semantics=("arbitrary",), vmem_limit_bytes=VMEM_LIMIT),
        name="ffn",
    )(x2d, pre, wg, wu, wd, post, *cast_weights)
    return outs[0], outs[1:]


def _head_norm(z, ones_bd, gain):
    z2 = z * z
    hi = z2.astype(BF16)
    lo = (z2 - hi.astype(F32)).astype(BF16)
    ssum = _dot(hi, ones_bd) + _dot(lo, ones_bd)
    return z * lax.rsqrt(ssum * (1.0 / HEAD_DIM) + EPS) * gain


def _rope(x, cos_t, sin_t, first_half):
    up = pltpu.roll(x, LANES - ROPE_NFREQ, axis=1)
    dn = pltpu.roll(x, ROPE_NFREQ, axis=1)
    return x * cos_t + jnp.where(first_half, up, dn) * sin_t


def _qkv_kernel(x_ref, pre_ref, w_ref, qg_ref, kg_ref, onesq_ref, onesk_ref,
                cos_ref, sin_ref, q_ref, kt_ref, v_ref):
    h = _rms(x_ref[...], pre_ref[...]).astype(BF16)
    z = _dot(h, w_ref[...])
    cos_t = cos_ref[...]
    sin_t = sin_ref[...]
    lane = lax.broadcasted_iota(jnp.int32, cos_t.shape, 1)
    first_half = (lane % (2 * ROPE_NFREQ)) < ROPE_NFREQ
    qn = _head_norm(z[:, :ATTN_WIDTH], onesq_ref[...], qg_ref[...])
    for j in range(N_SLABS):
        q_ref[j] = _rope(qn[:, j * LANES:(j + 1) * LANES], cos_t, sin_t, first_half).astype(BF16)
    kn = _head_norm(z[:, ATTN_WIDTH:ATTN_WIDTH + KV_WIDTH], onesk_ref[...], kg_ref[...])
    kr = _rope(kn, cos_t, sin_t, first_half)
    kt = kr.T.astype(BF16)
    kt_ref[0] = kt[:HEAD_DIM]
    kt_ref[1] = kt[HEAD_DIM:]
    v_ref[...] = z[:, ATTN_WIDTH + KV_WIDTH:].astype(BF16)


def _qkv(x3, pre, w_qkv, qg, kg, ones_q, ones_k, cos_t, sin_t):
    b, s, _ = x3.shape
    tm = QKV_TM
    return pl.pallas_call(
        _qkv_kernel,
        out_shape=(
            jax.ShapeDtypeStruct((b, N_SLABS, s, LANES), BF16),
            jax.ShapeDtypeStruct((b, N_KV_HEADS, HEAD_DIM, s), BF16),
            jax.ShapeDtypeStruct((b, s, KV_WIDTH), BF16),
        ),
        grid=(b, s // tm),
        in_specs=[
            pl.BlockSpec((None, tm, D_MODEL), lambda bi, i: (bi, i, 0)),
            _const_spec((1, D_MODEL)),
            _const_spec((D_MODEL, QKV_WIDTH)),
            _const_spec((1, ATTN_WIDTH)),
            _const_spec((1, KV_WIDTH)),
            _const_spec((ATTN_WIDTH, ATTN_WIDTH)),
            _const_spec((KV_WIDTH, KV_WIDTH)),
            pl.BlockSpec((tm, LANES), lambda bi, i: (i, 0)),
            pl.BlockSpec((tm, LANES), lambda bi, i: (i, 0)),
        ],
        out_specs=(
            pl.BlockSpec((None, N_SLABS, tm, LANES), lambda bi, i: (bi, 0, i, 0)),
            pl.BlockSpec((None, N_KV_HEADS, HEAD_DIM, tm), lambda bi, i: (bi, 0, 0, i)),
            pl.BlockSpec((None, tm, KV_WIDTH), lambda bi, i: (bi, i, 0)),
        ),
        compiler_params=pltpu.CompilerParams(
            dimension_semantics=("arbitrary", "arbitrary"), vmem_limit_bytes=VMEM_LIMIT),
        name="qkv",
    )(x3, pre, w_qkv, qg, kg, ones_q, ones_k, cos_t, sin_t)


def _memkv_kernel(m_ref, g_ref, w_ref, k_ref, v_ref):
    mn = _rms(m_ref[...], g_ref[...]).astype(BF16)
    kv = _dot(mn, w_ref[...])
    k_ref[...] = kv[:, :MEM_WIDTH].astype(BF16)
    v_ref[...] = kv[:, MEM_WIDTH:].astype(BF16)


def _memkv(mem, g, w):
    b = mem.shape[0]
    return pl.pallas_call(
        _memkv_kernel,
        out_shape=(
            jax.ShapeDtypeStruct((b, MEM_LEN, MEM_WIDTH), BF16),
            jax.ShapeDtypeStruct((b, MEM_LEN, MEM_WIDTH), BF16),
        ),
        grid=(b,),
        in_specs=[
            pl.BlockSpec((None, MEM_LEN, D_MODEL), lambda bi: (bi, 0, 0)),
            _const_spec((1, D_MODEL)),
            _const_spec((D_MODEL, 2 * MEM_WIDTH)),
        ],
        out_specs=(
            pl.BlockSpec((None, MEM_LEN, MEM_WIDTH), lambda bi: (bi, 0, 0)),
            pl.BlockSpec((None, MEM_LEN, MEM_WIDTH), lambda bi: (bi, 0, 0)),
        ),
        compiler_params=pltpu.CompilerParams(dimension_semantics=("arbitrary",)),
        name="memkv",
    )(mem, g, w)


def _attn_kernel(n_cast, q_ref, kt_ref, v_ref, *refs):
    o_ref = refs[n_cast]
    for w_ref, wb_ref in zip(refs[:n_cast], refs[n_cast + 1:]):
        wb_ref[...] = w_ref[...].astype(BF16)

    tq = q_ref.shape[1]
    seq = v_ref.shape[0]
    lane = lax.broadcasted_iota(jnp.int32, (tq, LANES), 1)
    lo_half = lane < HEAD_DIM
    ones = jnp.ones((ATT_KC, LANES), BF16)
    grp = pl.program_id(1)
    for j in range(q_ref.shape[0]):
        q = q_ref[j]
        zero = jnp.zeros_like(q)
        halves = []
        for par in range(2):
            lhs = jnp.where(lo_half, q, zero) if par == 0 else jnp.where(lo_half, zero, q)
            m = acc = None
            for c in range(seq // ATT_KC):
                ksl = slice(c * ATT_KC, (c + 1) * ATT_KC)
                kt = kt_ref[:, ksl]
                s = _dot(lhs, jnp.concatenate([kt, kt], axis=0))
                v_ext = jnp.concatenate([v_ref[ksl, :], ones], axis=1)
                mc = jnp.max(s, axis=-1, keepdims=True)
                if c == 0:
                    m = mc
                    acc = _dot(jnp.exp2(s - m).astype(BF16), v_ext)
                else:
                    m_new = jnp.maximum(m, mc)
                    acc = jnp.exp2(m - m_new) * acc + _dot(jnp.exp2(s - m_new).astype(BF16), v_ext)
                    m = m_new
            o = acc[:, :LANES] / acc[:, LANES:]
            o_sw = pltpu.roll(o, HEAD_DIM, axis=1)
            halves.append(jnp.where(grp == par, o, o_sw))
        o_ref[j] = jnp.where(lo_half, halves[0], halves[1]).astype(BF16)


def _attn(q4, kt, v, layer, weights):
    b, _, s, _ = q4.shape
    tq = ATT_TQ
    per_grp = N_SLABS // N_KV_HEADS
    n_tiles = s // tq
    w_in_specs, w_out_specs, w_out_shapes = _cast_specs(
        weights, layer, b * N_KV_HEADS * n_tiles,
        lambda bi, g, i: (bi * N_KV_HEADS + g) * n_tiles + i)
    outs = pl.pallas_call(
        functools.partial(_attn_kernel, len(weights)),
        out_shape=[jax.ShapeDtypeStruct((b, N_SLABS, s, LANES), BF16)] + w_out_shapes,
        grid=(b, N_KV_HEADS, n_tiles),
        in_specs=[
            pl.BlockSpec((None, per_grp, tq, LANES), lambda bi, g, i: (bi, g, i, 0)),
            pl.BlockSpec((None, None, HEAD_DIM, s), lambda bi, g, i: (bi, g, 0, 0)),
            pl.BlockSpec((None, s, KV_WIDTH), lambda bi, g, i: (bi, 0, 0)),
        ] + w_in_specs,
        out_specs=[pl.BlockSpec((None, per_grp, tq, LANES), lambda bi, g, i: (bi, g, i, 0))]
        + w_out_specs,
        compiler_params=pltpu.CompilerParams(
            dimension_semantics=("arbitrary", "arbitrary", "arbitrary"),
            vmem_limit_bytes=VMEM_LIMIT),
        name="attn",
    )(q4, kt, v, *weights)
    return outs[0], outs[1:]


def _mixer_kernel(x_ref, a_ref, pre_ref, wra_ref, wrb_ref, wbg_ref, bbg_ref, gvn_ref, ws_ref,
                  bs_ref, km_ref, vm_ref, wpa_ref, wpg_ref, wpm_ref, wo_ref, post_ref, o_ref):
    x = x_ref[...]
    tm = x.shape[0]
    nch = tm // GMLP_CHUNK
    gd = GMLP_WIDTH // GMLP_GROUPS
    hb = _rms(x, pre_ref[...]).astype(BF16)
    z = jnp.concatenate([_dot(hb, wra_ref[...]), _dot(hb, wrb_ref[...])], axis=1)
    gu = jax.nn.gelu(z[:, :GMLP_WIDTH])
    gv = jax.nn.gelu(z[:, GMLP_WIDTH:2 * GMLP_WIDTH])
    qm = z[:, 2 * GMLP_WIDTH:].astype(BF16)

    vn = _rms(gv, gvn_ref[...]).astype(BF16)
    cols = []
    for g in range(GMLP_GROUPS):
        rhs = jnp.concatenate(
            [vn[c * GMLP_CHUNK:(c + 1) * GMLP_CHUNK, g * gd:(g + 1) * gd] for c in range(nch)], axis=1)
        mixed = _dot(ws_ref[g], rhs) + jnp.concatenate([bs_ref[g]] * nch, axis=1)
        cols.append(jnp.concatenate(
            [mixed[:, c * gd:(c + 1) * gd] for c in range(nch)], axis=0))
    gm = (gu * jnp.concatenate(cols, axis=1)).astype(BF16)

    outs = []
    for hh in range(MEM_HEADS):
        sl = slice(hh * MEM_HEAD_DIM, (hh + 1) * MEM_HEAD_DIM)
        s = lax.dot_general(qm[:, sl], km_ref[:, sl], (((1,), (1,)), ((), ())),
                            preferred_element_type=F32) * (MEM_HEAD_DIM ** -0.5)
        m = jnp.max(s, axis=-1, keepdims=True)
        p = jnp.exp(s - m)
        l = jnp.sum(p, axis=-1, keepdims=True)
        outs.append(_dot(p.astype(BF16), vm_ref[:, sl]) / l)
    mo = jnp.concatenate(outs, axis=1).astype(BF16)

    att = jnp.concatenate([a_ref[j] for j in range(N_SLABS)], axis=1)
    gates = jax.nn.sigmoid(_dot(hb, wbg_ref[...]) + bbg_ref[...])
    merged = (gates[:, :D_MODEL] * _dot(att, wpa_ref[...])
              + gates[:, D_MODEL:2 * D_MODEL] * _dot(gm, wpg_ref[...])
              + gates[:, 2 * D_MODEL:] * _dot(mo, wpm_ref[...]))
    out = _dot(merged.astype(BF16), wo_ref[...])
    o_ref[...] = x + _rms(out, post_ref[...])


def _mixer(x3, a4, pre, w_in_b, w_bg, b_bg, gvn, ws, bs, km, vm, wpa, wpg, wpm, wo, post):
    b, s, _ = x3.shape
    tm = MIX_TM
    gd = GMLP_WIDTH // GMLP_GROUPS
    w_in_block = lambda j: pl.BlockSpec((D_MODEL, QKV_WIDTH), lambda bi, i: (0, j),
                                        pipeline_mode=pl.Buffered(1))
    return pl.pallas_call(
        _mixer_kernel,
        out_shape=jax.ShapeDtypeStruct((b, s, D_MODEL), F32),
        grid=(b, s // tm),
        in_specs=[
            pl.BlockSpec((None, tm, D_MODEL), lambda bi, i: (bi, i, 0)),
            pl.BlockSpec((None, N_SLABS, tm, LANES), lambda bi, i: (bi, 0, i, 0)),
            _const_spec((1, D_MODEL)),
            w_in_block(1),
            w_in_block(2),
            _const_spec((D_MODEL, 3 * D_MODEL)),
            _const_spec((1, 3 * D_MODEL)),
            _const_spec((1, GMLP_WIDTH)),
            _const_spec((GMLP_GROUPS, GMLP_CHUNK, GMLP_CHUNK)),
            _const_spec((GMLP_GROUPS, GMLP_CHUNK, gd)),
            pl.BlockSpec((None, MEM_LEN, MEM_WIDTH), lambda bi, i: (bi, 0, 0)),
            pl.BlockSpec((None, MEM_LEN, MEM_WIDTH), lambda bi, i: (bi, 0, 0)),
            _const_spec((ATTN_WIDTH, D_MODEL)),
            _const_spec((GMLP_WIDTH, D_MODEL)),
            _const_spec((MEM_WIDTH, D_MODEL)),
            _const_spec((D_MODEL, D_MODEL)),
            _const_spec((1, D_MODEL)),
        ],
        out_specs=pl.BlockSpec((None, tm, D_MODEL), lambda bi, i: (bi, i, 0)),
        compiler_params=pltpu.CompilerParams(
            dimension_semantics=("arbitrary", "arbitrary"), vmem_limit_bytes=VMEM_LIMIT),
        name="mixer",
    )(x3, a4, pre, w_in_b, w_in_b, w_bg, b_bg, gvn, ws, bs, km, vm, wpa, wpg, wpm, wo, post)


def _rope_tables(seq):
    f32 = np.float32
    rows = seq // GRID_W
    row = np.repeat(np.arange(rows, dtype=f32), GRID_W)
    col = np.tile(np.arange(GRID_W, dtype=f32), rows)
    inv_freq = (f32(ROPE_THETA) ** (-np.arange(ROPE_NFREQ, dtype=f32) / f32(ROPE_NFREQ))).astype(f32)
    ang = np.stack([row[:, None] * inv_freq, col[:, None] * inv_freq], axis=1)
    cos, sin = np.cos(ang).astype(f32), np.sin(ang).astype(f32)
    cos64 = np.concatenate([cos[:, 0], cos[:, 0], cos[:, 1], cos[:, 1]], axis=1)
    sin64 = np.concatenate([-sin[:, 0], sin[:, 0], -sin[:, 1], sin[:, 1]], axis=1)
    reps = LANES // HEAD_DIM
    return jnp.asarray(np.tile(cos64, (1, reps))), jnp.asarray(np.tile(sin64, (1, reps)))


def _block_ones(width):
    idx = np.arange(width) // HEAD_DIM
    return jnp.asarray((idx[:, None] == idx[None, :]).astype(np.float32), dtype=BF16)


def kernel(x, mem, ffn1_pre, ffn1_w_gate, ffn1_w_up, ffn1_w_down, ffn1_post, mix_pre, mem_norm, w_in, w_mem_kv, q_norm, k_norm, gmlp_v_norm, gmlp_w_s, gmlp_b_s, w_branch_gate, b_branch_gate, w_proj_attn, w_proj_gmlp, w_proj_mem, w_out, mix_post, ffn2_pre, ffn2_w_gate, ffn2_w_up, ffn2_w_down, ffn2_post):
    b, s, d = x.shape
    depth = w_in.shape[0]
    cos_t, sin_t = _rope_tables(s)
    ones_q = _block_ones(ATTN_WIDTH)
    ones_k = _block_ones(KV_WIDTH)
    gd = GMLP_WIDTH // GMLP_GROUPS
    row = lambda v: v.reshape(1, -1)
    bf = lambda w: w.astype(BF16)

    for l in range(depth):
        x, (w_in_b, w_mkv_b, w_bg_b, wpa_b, wpg_b, wpm_b, wo_b) = _ffn(
            x.reshape(b * s, d), row(ffn1_pre[l]), ffn1_w_gate, ffn1_w_up, ffn1_w_down,
            row(ffn1_post[l]), own_layer=l, cast_layer=l,
            cast_weights=(w_in, w_mem_kv, w_branch_gate, w_proj_attn, w_proj_gmlp, w_proj_mem,
                          w_out))
        x = x.reshape(b, s, d)

        qg = row(jnp.tile(q_norm[l], N_Q_HEADS)) * (HEAD_DIM ** -0.5 * np.log2(np.e))
        kg = row(jnp.tile(k_norm[l], N_KV_HEADS))
        q4, kt, v = _qkv(x, row(mix_pre[l]), w_in_b, qg, kg, ones_q, ones_k, cos_t, sin_t)
        km, vm = _memkv(mem, row(mem_norm[l]), w_mkv_b)
        a4, (wg2_b, wu2_b, wd2_b) = _attn(q4, kt, v, l, (ffn2_w_gate, ffn2_w_up, ffn2_w_down))
        bs = jnp.broadcast_to(gmlp_b_s[l][:, :, None], (GMLP_GROUPS, GMLP_CHUNK, gd))
        x = _mixer(x, a4, row(mix_pre[l]), w_in_b, w_bg_b, row(b_branch_gate[l]),
                   row(gmlp_v_norm[l]), bf(gmlp_w_s[l]), bs, km, vm, wpa_b, wpg_b, wpm_b, wo_b,
                   row(mix_post[l]))

        x, _ = _ffn(x.reshape(b * s, d), row(ffn2_pre[l]), wg2_b, wu2_b, wd2_b, row(ffn2_post[l]))
        x = x.reshape(b, s, d)
    return x
```

```python
import functools

import numpy as np
import jax
import jax.numpy as jnp
from jax import lax
from jax.experimental import pallas as pl
from jax.experimental.pallas import tpu as pltpu

D_MODEL = 1024
MEM_LEN = 256
GRID_W = 64
EPS = 1e-6
HEAD_DIM = 64
ATTN_WIDTH = 512
N_Q_HEADS = 8
N_KV_HEADS = 2
KV_WIDTH = 128
QKV_WIDTH = ATTN_WIDTH + 2 * KV_WIDTH
ROPE_THETA = 10000.0
ROPE_NFREQ = 16
GMLP_WIDTH = 512
GMLP_GROUPS = 4
GMLP_CHUNK = 128
MEM_HEADS = 4
MEM_HEAD_DIM = 128
MEM_WIDTH = 512
D_FF = 2816

LANES = 128
BF16_SUBLANES = 16
N_SLABS = ATTN_WIDTH // LANES
VMEM_LIMIT = 56 * 1024 * 1024

FFN_TM = 1024
FFN_SUB = 512
FFN_FC = 256
FFN_STAGE_ROWS = 64
FFN_STAGE_SLOTS = 3
QKV_TM = 2048
QKV_SUB = 512
ATT_TQ = 1024
ATT_KC = 256
MIX_TM = 512

BF16 = jnp.bfloat16
F32 = jnp.float32


def _rms(x, g):
    ms = jnp.mean(x * x, axis=-1, keepdims=True)
    return x * lax.rsqrt(ms + EPS) * g


def _dot(a, b):
    return jnp.dot(a, b, preferred_element_type=F32)


def _const_spec(shape):
    nd = len(shape)
    return pl.BlockSpec(shape, lambda *_: (0,) * nd, pipeline_mode=pl.Buffered(1))


def _cast_block_rows(n_rows, n_steps):
    for rows in range(BF16_SUBLANES, n_rows + 1, BF16_SUBLANES):
        if n_rows % rows == 0 and n_rows // rows <= n_steps:
            return rows
    raise ValueError((n_rows, n_steps))


def _cast_specs(weights, layer, n_grid, step_of):
    in_specs, out_specs, out_shapes = [], [], []
    for w in weights:
        _, n_rows, n_cols = w.shape
        rows = _cast_block_rows(n_rows, n_grid)
        last = n_rows // rows - 1

        def blk(*ids, last=last):
            return jnp.minimum(step_of(*ids), last)

        in_specs.append(pl.BlockSpec((None, rows, n_cols), lambda *ids, blk=blk: (layer, blk(*ids), 0)))
        out_specs.append(pl.BlockSpec((rows, n_cols), lambda *ids, blk=blk: (blk(*ids), 0)))
        out_shapes.append(jax.ShapeDtypeStruct((n_rows, n_cols), BF16))
    return in_specs, out_specs, out_shapes


def _stage_copy(src_hbm, stg_ref, sem, job, c):
    slots, rows = stg_ref.shape[0], stg_ref.shape[1]
    return pltpu.make_async_copy(src_hbm.at[pl.ds(c * rows, rows), :], stg_ref.at[c % slots],
                                 sem.at[job, c % slots])


def _stage_weights(jobs, sem):
    n_chunks = [src.shape[0] // stg.shape[1] for src, _, stg in jobs]
    for k, (src, _, stg) in enumerate(jobs):
        for c in range(min(stg.shape[0], n_chunks[k])):
            _stage_copy(src, stg, sem, k, c).start()
    for c in range(max(n_chunks)):
        for k, (src, dst, stg) in enumerate(jobs):
            if c >= n_chunks[k]:
                continue
            slots, rows = stg.shape[0], stg.shape[1]
            _stage_copy(src, stg, sem, k, c).wait()
            dst[c * rows:(c + 1) * rows, :] = stg[c % slots].astype(BF16)
            if c + slots < n_chunks[k]:
                _stage_copy(src, stg, sem, k, c + slots).start()


def _ffn_kernel(own_layer, n_cast, x_ref, pre_ref, wg_in, wu_in, wd_in, post_ref, *refs):
    o_ref = refs[n_cast]
    if own_layer is None:
        wg_ref, wu_ref, wd_ref = wg_in, wu_in, wd_in
    else:
        wg_ref, wu_ref, wd_ref, stg_a, stg_b, stg_d, sem = refs[2 * n_cast + 1:]

        @pl.when(pl.program_id(0) == 0)
        def _():
            _stage_weights([(wg_in.at[own_layer], wg_ref, stg_a),
                            (wu_in.at[own_layer], wu_ref, stg_b),
                            (wd_in.at[own_layer], wd_ref, stg_d)], sem)

    for w_ref, wb_ref in zip(refs[:n_cast], refs[n_cast + 1:2 * n_cast + 1]):
        wb_ref[...] = w_ref[...].astype(BF16)

    n_sub = x_ref.shape[0] // FFN_SUB
    rows = [slice(r * FFN_SUB, (r + 1) * FFN_SUB) for r in range(n_sub)]
    hs = [_rms(x_ref[r, :], pre_ref[...]).astype(BF16) for r in rows]
    accs = [None] * n_sub
    for c in range(D_FF // FFN_FC):
        sl = slice(c * FFN_FC, (c + 1) * FFN_FC)
        for k in range(n_sub):
            g = _dot(hs[k], wg_ref[:, sl])
            u = _dot(hs[k], wu_ref[:, sl])
            a = (g * jax.nn.sigmoid(g) * u).astype(BF16)
            d = _dot(a, wd_ref[sl, :])
            accs[k] = d if c == 0 else accs[k] + d
    for k, r in enumerate(rows):
        o_ref[r, :] = x_ref[r, :] + 0.5 * _rms(accs[k], post_ref[...])


def _ffn(x2d, pre, wg, wu, wd, post, own_layer=None, cast_layer=0, cast_weights=()):
    t = x2d.shape[0]
    n_grid = t // FFN_TM
    c_in, c_out, c_shapes = _cast_specs(cast_weights, cast_layer, n_grid, lambda i: i)
    if own_layer is None:
        w_specs = [_const_spec((D_MODEL, D_FF)), _const_spec((D_MODEL, D_FF)),
                   _const_spec((D_FF, D_MODEL))]
        scratch = []
    else:
        w_specs = [pl.BlockSpec(memory_space=pl.ANY)] * 3
        scratch = [
            pltpu.VMEM((D_MODEL, D_FF), BF16),
            pltpu.VMEM((D_MODEL, D_FF), BF16),
            pltpu.VMEM((D_FF, D_MODEL), BF16),
            pltpu.VMEM((FFN_STAGE_SLOTS, FFN_STAGE_ROWS, D_FF), F32),
            pltpu.VMEM((FFN_STAGE_SLOTS, FFN_STAGE_ROWS, D_FF), F32),
            pltpu.VMEM((FFN_STAGE_SLOTS, FFN_STAGE_ROWS * D_FF // D_MODEL, D_MODEL), F32),
            pltpu.SemaphoreType.DMA((3, FFN_STAGE_SLOTS)),
        ]
    outs = pl.pallas_call(
        functools.partial(_ffn_kernel, own_layer, len(cast_weights)),
        out_shape=[jax.ShapeDtypeStruct((t, D_MODEL), F32)] + c_shapes,
        grid=(n_grid,),
        in_specs=[pl.BlockSpec((FFN_TM, D_MODEL), lambda i: (i, 0)), _const_spec((1, D_MODEL))]
        + w_specs + [_const_spec((1, D_MODEL))] + c_in,
        out_specs=[pl.BlockSpec((FFN_TM, D_MODEL), lambda i: (i, 0))] + c_out,
        scratch_shapes=scratch,
        compiler_params=pltpu.CompilerParams(
            dimension_semantics=("arbitrary",), vmem_limit_bytes=VMEM_LIMIT),
        name="ffn",
    )(x2d, pre, wg, wu, wd, post, *cast_weights)
    return outs[0], outs[1:]


def _head_norm(z, ones_bd, gain):
    z2 = z * z
    hi = z2.astype(BF16)
    lo = (z2 - hi.astype(F32)).astype(BF16)
    ssum = _dot(hi, ones_bd) + _dot(lo, ones_bd)
    return z * lax.rsqrt(ssum * (1.0 / HEAD_DIM) + EPS) * gain


def _rope(x, cos_t, sin_t, first_half):
    up = pltpu.roll(x, LANES - ROPE_NFREQ, axis=1)
    dn = pltpu.roll(x, ROPE_NFREQ, axis=1)
    return x * cos_t + jnp.where(first_half, up, dn) * sin_t


def _qkv_kernel(x_ref, pre_ref, w_ref, qg_ref, kg_ref, onesq_ref, onesk_ref,
                cos_ref, sin_ref, q_ref, kt_ref, v_ref):
    lane = lax.broadcasted_iota(jnp.int32, (QKV_SUB, LANES), 1)
    first_half = (lane % (2 * ROPE_NFREQ)) < ROPE_NFREQ
    for r0 in range(0, x_ref.shape[0], QKV_SUB):
        r = slice(r0, r0 + QKV_SUB)
        h = _rms(x_ref[r, :], pre_ref[...]).astype(BF16)
        z = _dot(h, w_ref[...])
        cos_t = cos_ref[r, :]
        sin_t = sin_ref[r, :]
        qn = _head_norm(z[:, :ATTN_WIDTH], onesq_ref[...], qg_ref[...])
        for j in range(N_SLABS):
            q_ref[j, r, :] = _rope(qn[:, j * LANES:(j + 1) * LANES], cos_t, sin_t,
                                   first_half).astype(BF16)
        kn = _head_norm(z[:, ATTN_WIDTH:ATTN_WIDTH + KV_WIDTH], onesk_ref[...], kg_ref[...])
        kr = _rope(kn, cos_t, sin_t, first_half)
        kt = kr.T.astype(BF16)
        kt_ref[0, :, r] = kt[:HEAD_DIM]
        kt_ref[1, :, r] = kt[HEAD_DIM:]
        v_ref[r, :] = z[:, ATTN_WIDTH + KV_WIDTH:].astype(BF16)


def _qkv(x3, pre, w_qkv, qg, kg, ones_q, ones_k, cos_t, sin_t):
    b, s, _ = x3.shape
    tm = QKV_TM
    return pl.pallas_call(
        _qkv_kernel,
        out_shape=(
            jax.ShapeDtypeStruct((b, N_SLABS, s, LANES), BF16),
            jax.ShapeDtypeStruct((b, N_KV_HEADS, HEAD_DIM, s), BF16),
            jax.ShapeDtypeStruct((b, s, KV_WIDTH), BF16),
        ),
        grid=(b, s // tm),
        in_specs=[
            pl.BlockSpec((None, tm, D_MODEL), lambda bi, i: (bi, i, 0)),
            _const_spec((1, D_MODEL)),
            _const_spec((D_MODEL, QKV_WIDTH)),
            _const_spec((1, ATTN_WIDTH)),
            _const_spec((1, KV_WIDTH)),
            _const_spec((ATTN_WIDTH, ATTN_WIDTH)),
            _const_spec((KV_WIDTH, KV_WIDTH)),
            pl.BlockSpec((tm, LANES), lambda bi, i: (i, 0)),
            pl.BlockSpec((tm, LANES), lambda bi, i: (i, 0)),
        ],
        out_specs=(
            pl.BlockSpec((None, N_SLABS, tm, LANES), lambda bi, i: (bi, 0, i, 0)),
            pl.BlockSpec((None, N_KV_HEADS, HEAD_DIM, tm), lambda bi, i: (bi, 0, 0, i)),
            pl.BlockSpec((None, tm, KV_WIDTH), lambda bi, i: (bi, i, 0)),
        ),
        compiler_params=pltpu.CompilerParams(
            dimension_semantics=("arbitrary", "arbitrary"), vmem_limit_bytes=VMEM_LIMIT),
        name="qkv",
    )(x3, pre, w_qkv, qg, kg, ones_q, ones_k, cos_t, sin_t)


def _memkv_kernel(m_ref, g_ref, w_ref, k_ref, v_ref):
    mn = _rms(m_ref[...], g_ref[...]).astype(BF16)
    kv = _dot(mn, w_ref[...])
    k_ref[...] = kv[:, :MEM_WIDTH].astype(BF16)
    v_ref[...] = kv[:, MEM_WIDTH:].astype(BF16)


def _memkv(mem, g, w):
    b = mem.shape[0]
    return pl.pallas_call(
        _memkv_kernel,
        out_shape=(
            jax.ShapeDtypeStruct((b, MEM_LEN, MEM_WIDTH), BF16),
            jax.ShapeDtypeStruct((b, MEM_LEN, MEM_WIDTH), BF16),
        ),
        grid=(b,),
        in_specs=[
            pl.BlockSpec((None, MEM_LEN, D_MODEL), lambda bi: (bi, 0, 0)),
            _const_spec((1, D_MODEL)),
            _const_spec((D_MODEL, 2 * MEM_WIDTH)),
        ],
        out_specs=(
            pl.BlockSpec((None, MEM_LEN, MEM_WIDTH), lambda bi: (bi, 0, 0)),
            pl.BlockSpec((None, MEM_LEN, MEM_WIDTH), lambda bi: (bi, 0, 0)),
        ),
        compiler_params=pltpu.CompilerParams(dimension_semantics=("arbitrary",)),
        name="memkv",
    )(mem, g, w)


def _attn_kernel(n_cast, q_ref, kt_ref, v_ref, *refs):
    o_ref = refs[n_cast]
    for w_ref, wb_ref in zip(refs[:n_cast], refs[n_cast + 1:]):
        wb_ref[...] = w_ref[...].astype(BF16)

    tq = q_ref.shape[1]
    seq = v_ref.shape[0]
    lane = lax.broadcasted_iota(jnp.int32, (tq, LANES), 1)
    lo_half = lane < HEAD_DIM
    ones = jnp.ones((ATT_KC, LANES), BF16)
    grp = pl.program_id(1)
    for j in range(q_ref.shape[0]):
        q = q_ref[j]
        zero = jnp.zeros_like(q)
        halves = []
        for par in range(2):
            lhs = jnp.where(lo_half, q, zero) if par == 0 else jnp.where(lo_half, zero, q)
            m = acc = None
            for c in range(seq // ATT_KC):
                ksl = slice(c * ATT_KC, (c + 1) * ATT_KC)
                kt = kt_ref[:, ksl]
                s = _dot(lhs, jnp.concatenate([kt, kt], axis=0))
                v_ext = jnp.concatenate([v_ref[ksl, :], ones], axis=1)
                mc = jnp.max(s, axis=-1, keepdims=True)
                if c == 0:
                    m = mc
                    acc = _dot(jnp.exp2(s - m).astype(BF16), v_ext)
                else:
                    m_new = jnp.maximum(m, mc)
                    acc = jnp.exp2(m - m_new) * acc + _dot(jnp.exp2(s - m_new).astype(BF16), v_ext)
                    m = m_new
            o = acc[:, :LANES] / acc[:, LANES:]
            o_sw = pltpu.roll(o, HEAD_DIM, axis=1)
            halves.append(jnp.where(grp == par, o, o_sw))
        o_ref[j] = jnp.where(lo_half, halves[0], halves[1]).astype(BF16)


def _attn(q4, kt, v, layer, weights):
    b, _, s, _ = q4.shape
    tq = ATT_TQ
    per_grp = N_SLABS // N_KV_HEADS
    n_tiles = s // tq
    w_in_specs, w_out_specs, w_out_shapes = _cast_specs(
        weights, layer, b * N_KV_HEADS * n_tiles,
        lambda bi, g, i: (bi * N_KV_HEADS + g) * n_tiles + i)
    outs = pl.pallas_call(
        functools.partial(_attn_kernel, len(weights)),
        out_shape=[jax.ShapeDtypeStruct((b, N_SLABS, s, LANES), BF16)] + w_out_shapes,
        grid=(b, N_KV_HEADS, n_tiles),
        in_specs=[
            pl.BlockSpec((None, per_grp, tq, LANES), lambda bi, g, i: (bi, g, i, 0)),
            pl.BlockSpec((None, None, HEAD_DIM, s), lambda bi, g, i: (bi, g, 0, 0)),
            pl.BlockSpec((None, s, KV_WIDTH), lambda bi, g, i: (bi, 0, 0)),
        ] + w_in_specs,
        out_specs=[pl.BlockSpec((None, per_grp, tq, LANES), lambda bi, g, i: (bi, g, i, 0))]
        + w_out_specs,
        compiler_params=pltpu.CompilerParams(
            dimension_semantics=("arbitrary", "arbitrary", "arbitrary"),
            vmem_limit_bytes=VMEM_LIMIT),
        name="attn",
    )(q4, kt, v, *weights)
    return outs[0], outs[1:]


def _mixer_kernel(x_ref, a_ref, pre_ref, wra_ref, wrb_ref, wbg_ref, bbg_ref, gvn_ref, ws_ref,
                  bs_ref, km_ref, vm_ref, wpa_ref, wpg_ref, wpm_ref, wo_ref, post_ref, o_ref):
    x = x_ref[...]
    tm = x.shape[0]
    nch = tm // GMLP_CHUNK
    gd = GMLP_WIDTH // GMLP_GROUPS
    hb = _rms(x, pre_ref[...]).astype(BF16)
    z = jnp.concatenate([_dot(hb, wra_ref[...]), _dot(hb, wrb_ref[...])], axis=1)
    gu = jax.nn.gelu(z[:, :GMLP_WIDTH])
    gv = jax.nn.gelu(z[:, GMLP_WIDTH:2 * GMLP_WIDTH])
    qm = z[:, 2 * GMLP_WIDTH:].astype(BF16)

    vn = _rms(gv, gvn_ref[...]).astype(BF16)
    cols = []
    for g in range(GMLP_GROUPS):
        rhs = jnp.concatenate(
            [vn[c * GMLP_CHUNK:(c + 1) * GMLP_CHUNK, g * gd:(g + 1) * gd] for c in range(nch)], axis=1)
        mixed = _dot(ws_ref[g], rhs) + jnp.concatenate([bs_ref[g]] * nch, axis=1)
        cols.append(jnp.concatenate(
            [mixed[:, c * gd:(c + 1) * gd] for c in range(nch)], axis=0))
    gm = (gu * jnp.concatenate(cols, axis=1)).astype(BF16)

    outs = []
    for hh in range(MEM_HEADS):
        sl = slice(hh * MEM_HEAD_DIM, (hh + 1) * MEM_HEAD_DIM)
        s = lax.dot_general(qm[:, sl], km_ref[:, sl], (((1,), (1,)), ((), ())),
                            preferred_element_type=F32) * (MEM_HEAD_DIM ** -0.5)
        m = jnp.max(s, axis=-1, keepdims=True)
        p = jnp.exp(s - m)
        l = jnp.sum(p, axis=-1, keepdims=True)
        outs.append(_dot(p.astype(BF16), vm_ref[:, sl]) / l)
    mo = jnp.concatenate(outs, axis=1).astype(BF16)

    att = jnp.concatenate([a_ref[j] for j in range(N_SLABS)], axis=1)
    gates = jax.nn.sigmoid(_dot(hb, wbg_ref[...]) + bbg_ref[...])
    merged = (gates[:, :D_MODEL] * _dot(att, wpa_ref[...])
              + gates[:, D_MODEL:2 * D_MODEL] * _dot(gm, wpg_ref[...])
              + gates[:, 2 * D_MODEL:] * _dot(mo, wpm_ref[...]))
    out = _dot(merged.astype(BF16), wo_ref[...])
    o_ref[...] = x + _rms(out, post_ref[...])


def _mixer(x3, a4, pre, w_in_b, w_bg, b_bg, gvn, ws, bs, km, vm, wpa, wpg, wpm, wo, post):
    b, s, _ = x3.shape
    tm = MIX_TM
    gd = GMLP_WIDTH // GMLP_GROUPS
    w_in_block = lambda j: pl.BlockSpec((D_MODEL, QKV_WIDTH), lambda bi, i: (0, j),
                                        pipeline_mode=pl.Buffered(1))
    return pl.pallas_call(
        _mixer_kernel,
        out_shape=jax.ShapeDtypeStruct((b, s, D_MODEL), F32),
        grid=(b, s // tm),
        in_specs=[
            pl.BlockSpec((None, tm, D_MODEL), lambda bi, i: (bi, i, 0)),
            pl.BlockSpec((None, N_SLABS, tm, LANES), lambda bi, i: (bi, 0, i, 0)),
            _const_spec((1, D_MODEL)),
            w_in_block(1),
            w_in_block(2),
            _const_spec((D_MODEL, 3 * D_MODEL)),
            _const_spec((1, 3 * D_MODEL)),
            _const_spec((1, GMLP_WIDTH)),
            _const_spec((GMLP_GROUPS, GMLP_CHUNK, GMLP_CHUNK)),
            _const_spec((GMLP_GROUPS, GMLP_CHUNK, gd)),
            pl.BlockSpec((None, MEM_LEN, MEM_WIDTH), lambda bi, i: (bi, 0, 0)),
            pl.BlockSpec((None, MEM_LEN, MEM_WIDTH), lambda bi, i: (bi, 0, 0)),
            _const_spec((ATTN_WIDTH, D_MODEL)),
            _const_spec((GMLP_WIDTH, D_MODEL)),
            _const_spec((MEM_WIDTH, D_MODEL)),
            _const_spec((D_MODEL, D_MODEL)),
            _const_spec((1, D_MODEL)),
        ],
        out_specs=pl.BlockSpec((None, tm, D_MODEL), lambda bi, i: (bi, i, 0)),
        compiler_params=pltpu.CompilerParams(
            dimension_semantics=("arbitrary", "arbitrary"), vmem_limit_bytes=VMEM_LIMIT),
        name="mixer",
    )(x3, a4, pre, w_in_b, w_in_b, w_bg, b_bg, gvn, ws, bs, km, vm, wpa, wpg, wpm, wo, post)


def _rope_tables(seq):
    f32 = np.float32
    rows = seq // GRID_W
    row = np.repeat(np.arange(rows, dtype=f32), GRID_W)
    col = np.tile(np.arange(GRID_W, dtype=f32), rows)
    inv_freq = (f32(ROPE_THETA) ** (-np.arange(ROPE_NFREQ, dtype=f32) / f32(ROPE_NFREQ))).astype(f32)
    ang = np.stack([row[:, None] * inv_freq, col[:, None] * inv_freq], axis=1)
    cos, sin = np.cos(ang).astype(f32), np.sin(ang).astype(f32)
    cos64 = np.concatenate([cos[:, 0], cos[:, 0], cos[:, 1], cos[:, 1]], axis=1)
    sin64 = np.concatenate([-sin[:, 0], sin[:, 0], -sin[:, 1], sin[:, 1]], axis=1)
    reps = LANES // HEAD_DIM
    return jnp.asarray(np.tile(cos64, (1, reps))), jnp.asarray(np.tile(sin64, (1, reps)))


def _block_ones(width):
    idx = np.arange(width) // HEAD_DIM
    return jnp.asarray((idx[:, None] == idx[None, :]).astype(np.float32), dtype=BF16)


def kernel(x, mem, ffn1_pre, ffn1_w_gate, ffn1_w_up, ffn1_w_down, ffn1_post, mix_pre, mem_norm, w_in, w_mem_kv, q_norm, k_norm, gmlp_v_norm, gmlp_w_s, gmlp_b_s, w_branch_gate, b_branch_gate, w_proj_attn, w_proj_gmlp, w_proj_mem, w_out, mix_post, ffn2_pre, ffn2_w_gate, ffn2_w_up, ffn2_w_down, ffn2_post):
    b, s, d = x.shape
    depth = w_in.shape[0]
    cos_t, sin_t = _rope_tables(s)
    ones_q = _block_ones(ATTN_WIDTH)
    ones_k = _block_ones(KV_WIDTH)
    gd = GMLP_WIDTH // GMLP_GROUPS
    row = lambda v: v.reshape(1, -1)
    bf = lambda w: w.astype(BF16)

    for l in range(depth):
        x, (w_in_b, w_mkv_b, w_bg_b, wpa_b, wpg_b, wpm_b, wo_b) = _ffn(
            x.reshape(b * s, d), row(ffn1_pre[l]), ffn1_w_gate, ffn1_w_up, ffn1_w_down,
            row(ffn1_post[l]), own_layer=l, cast_layer=l,
            cast_weights=(w_in, w_mem_kv, w_branch_gate, w_proj_attn, w_proj_gmlp, w_proj_mem,
                          w_out))
        x = x.reshape(b, s, d)

        qg = row(jnp.tile(q_norm[l], N_Q_HEADS)) * (HEAD_DIM ** -0.5 * np.log2(np.e))
        kg = row(jnp.tile(k_norm[l], N_KV_HEADS))
        q4, kt, v = _qkv(x, row(mix_pre[l]), w_in_b, qg, kg, ones_q, ones_k, cos_t, sin_t)
        km, vm = _memkv(mem, row(mem_norm[l]), w_mkv_b)
        a4, (wg2_b, wu2_b, wd2_b) = _attn(q4, kt, v, l, (ffn2_w_gate, ffn2_w_up, ffn2_w_down))
        bs = jnp.broadcast_to(gmlp_b_s[l][:, :, None], (GMLP_GROUPS, GMLP_CHUNK, gd))
        x = _mixer(x, a4, row(mix_pre[l]), w_in_b, w_bg_b, row(b_branch_gate[l]),
                   row(gmlp_v_norm[l]), bf(gmlp_w_s[l]), bs, km, vm, wpa_b, wpg_b, wpm_b, wo_b,
                   row(mix_post[l]))

        x, _ = _ffn(x.reshape(b * s, d), row(ffn2_pre[l]), wg2_b, wu2_b, wd2_b, row(ffn2_post[l]))
        x = x.reshape(b, s, d)
    return x
```

```python
import functools

import numpy as np
import jax
import jax.numpy as jnp
from jax import lax
from jax.experimental import pallas as pl
from jax.experimental.pallas import tpu as pltpu

D_MODEL = 1024
MEM_LEN = 256
GRID_W = 64
EPS = 1e-6
HEAD_DIM = 64
ATTN_WIDTH = 512
N_Q_HEADS = 8
N_KV_HEADS = 2
KV_WIDTH = 128
QKV_WIDTH = ATTN_WIDTH + 2 * KV_WIDTH
ROPE_THETA = 10000.0
ROPE_NFREQ = 16
GMLP_WIDTH = 512
GMLP_GROUPS = 4
GMLP_CHUNK = 128
MEM_HEADS = 4
MEM_HEAD_DIM = 128
MEM_WIDTH = 512
D_FF = 2816

LANES = 128
BF16_SUBLANES = 16
N_SLABS = ATTN_WIDTH // LANES
VMEM_LIMIT = 56 * 1024 * 1024

FFN_TM = 1024
FFN_SUB = 512
FFN_FC = 256
FFN_STAGE_ROWS = 64
FFN_STAGE_SLOTS = 3
QKV_TM = 2048
QKV_SUB = 512
ATT_TQ = 1024
ATT_KC = 256
MIX_TM = 1024
MIX_SUB = 512

BF16 = jnp.bfloat16
F32 = jnp.float32


def _rms(x, g):
    ms = jnp.mean(x * x, axis=-1, keepdims=True)
    return x * lax.rsqrt(ms + EPS) * g


def _dot(a, b):
    return jnp.dot(a, b, preferred_element_type=F32)


def _const_spec(shape):
    nd = len(shape)
    return pl.BlockSpec(shape, lambda *_: (0,) * nd, pipeline_mode=pl.Buffered(1))


def _cast_block_rows(n_rows, n_steps):
    for rows in range(BF16_SUBLANES, n_rows + 1, BF16_SUBLANES):
        if n_rows % rows == 0 and n_rows // rows <= n_steps:
            return rows
    raise ValueError((n_rows, n_steps))


def _cast_specs(weights, layer, n_grid, step_of):
    in_specs, out_specs, out_shapes = [], [], []
    for w in weights:
        _, n_rows, n_cols = w.shape
        rows = _cast_block_rows(n_rows, n_grid)
        last = n_rows // rows - 1

        def blk(*ids, last=last):
            return jnp.minimum(step_of(*ids), last)

        in_specs.append(pl.BlockSpec((None, rows, n_cols), lambda *ids, blk=blk: (layer, blk(*ids), 0)))
        out_specs.append(pl.BlockSpec((rows, n_cols), lambda *ids, blk=blk: (blk(*ids), 0)))
        out_shapes.append(jax.ShapeDtypeStruct((n_rows, n_cols), BF16))
    return in_specs, out_specs, out_shapes


def _stage_copy(src_hbm, stg_ref, sem, job, c):
    slots, rows = stg_ref.shape[0], stg_ref.shape[1]
    return pltpu.make_async_copy(src_hbm.at[pl.ds(c * rows, rows), :], stg_ref.at[c % slots],
                                 sem.at[job, c % slots])


def _stage_weights(jobs, sem):
    n_chunks = [src.shape[0] // stg.shape[1] for src, _, stg in jobs]
    for k, (src, _, stg) in enumerate(jobs):
        for c in range(min(stg.shape[0], n_chunks[k])):
            _stage_copy(src, stg, sem, k, c).start()
    for c in range(max(n_chunks)):
        for k, (src, dst, stg) in enumerate(jobs):
            if c >= n_chunks[k]:
                continue
            slots, rows = stg.shape[0], stg.shape[1]
            _stage_copy(src, stg, sem, k, c).wait()
            dst[c * rows:(c + 1) * rows, :] = stg[c % slots].astype(BF16)
            if c + slots < n_chunks[k]:
                _stage_copy(src, stg, sem, k, c + slots).start()


def _ffn_kernel(own_layer, n_cast, x_ref, pre_ref, wg_in, wu_in, wd_in, post_ref, *refs):
    o_ref = refs[n_cast]
    if own_layer is None:
        wg_ref, wu_ref, wd_ref = wg_in, wu_in, wd_in
    else:
        wg_ref, wu_ref, wd_ref, stg_a, stg_b, stg_d, sem = refs[2 * n_cast + 1:]

        @pl.when(pl.program_id(0) == 0)
        def _():
            _stage_weights([(wg_in.at[own_layer], wg_ref, stg_a),
                            (wu_in.at[own_layer], wu_ref, stg_b),
                            (wd_in.at[own_layer], wd_ref, stg_d)], sem)

    for w_ref, wb_ref in zip(refs[:n_cast], refs[n_cast + 1:2 * n_cast + 1]):
        wb_ref[...] = w_ref[...].astype(BF16)

    n_sub = x_ref.shape[0] // FFN_SUB
    rows = [slice(r * FFN_SUB, (r + 1) * FFN_SUB) for r in range(n_sub)]
    hs = [_rms(x_ref[r, :], pre_ref[...]).astype(BF16) for r in rows]
    accs = [None] * n_sub
    for c in range(D_FF // FFN_FC):
        sl = slice(c * FFN_FC, (c + 1) * FFN_FC)
        for k in range(n_sub):
            g = _dot(hs[k], wg_ref[:, sl])
            u = _dot(hs[k], wu_ref[:, sl])
            a = (g * jax.nn.sigmoid(g) * u).astype(BF16)
            d = _dot(a, wd_ref[sl, :])
            accs[k] = d if c == 0 else accs[k] + d
    for k, r in enumerate(rows):
        o_ref[r, :] = x_ref[r, :] + 0.5 * _rms(accs[k], post_ref[...])


def _ffn(x2d, pre, wg, wu, wd, post, own_layer=None, cast_layer=0, cast_weights=()):
    t = x2d.shape[0]
    n_grid = t // FFN_TM
    c_in, c_out, c_shapes = _cast_specs(cast_weights, cast_layer, n_grid, lambda i: i)
    if own_layer is None:
        w_specs = [_const_spec((D_MODEL, D_FF)), _const_spec((D_MODEL, D_FF)),
                   _const_spec((D_FF, D_MODEL))]
        scratch = []
    else:
        w_specs = [pl.BlockSpec(memory_space=pl.ANY)] * 3
        scratch = [
            pltpu.VMEM((D_MODEL, D_FF), BF16),
            pltpu.VMEM((D_MODEL, D_FF), BF16),
            pltpu.VMEM((D_FF, D_MODEL), BF16),
            pltpu.VMEM((FFN_STAGE_SLOTS, FFN_STAGE_ROWS, D_FF), F32),
            pltpu.VMEM((FFN_STAGE_SLOTS, FFN_STAGE_ROWS, D_FF), F32),
            pltpu.VMEM((FFN_STAGE_SLOTS, FFN_STAGE_ROWS * D_FF // D_MODEL, D_MODEL), F32),
            pltpu.SemaphoreType.DMA((3, FFN_STAGE_SLOTS)),
        ]
    outs = pl.pallas_call(
        functools.partial(_ffn_kernel, own_layer, len(cast_weights)),
        out_shape=[jax.ShapeDtypeStruct((t, D_MODEL), F32)] + c_shapes,
        grid=(n_grid,),
        in_specs=[pl.BlockSpec((FFN_TM, D_MODEL), lambda i: (i, 0)), _const_spec((1, D_MODEL))]
        + w_specs + [_const_spec((1, D_MODEL))] + c_in,
        out_specs=[pl.BlockSpec((FFN_TM, D_MODEL), lambda i: (i, 0))] + c_out,
        scratch_shapes=scratch,
        compiler_params=pltpu.CompilerParams(
            dimension_semantics=("arbitrary",), vmem_limit_bytes=VMEM_LIMIT),
        name="ffn",
    )(x2d, pre, wg, wu, wd, post, *cast_weights)
    return outs[0], outs[1:]


def _head_norm(z, ones_bd, gain):
    z2 = z * z
    hi = z2.astype(BF16)
    lo = (z2 - hi.astype(F32)).astype(BF16)
    ssum = _dot(hi, ones_bd) + _dot(lo, ones_bd)
    return z * lax.rsqrt(ssum * (1.0 / HEAD_DIM) + EPS) * gain


def _rope(x, cos_t, sin_t, first_half):
    up = pltpu.roll(x, LANES - ROPE_NFREQ, axis=1)
    dn = pltpu.roll(x, ROPE_NFREQ, axis=1)
    return x * cos_t + jnp.where(first_half, up, dn) * sin_t


def _qkv_kernel(x_ref, pre_ref, w_ref, qg_ref, kg_ref, onesq_ref, onesk_ref,
                cos_ref, sin_ref, q_ref, kt_ref, v_ref):
    lane = lax.broadcasted_iota(jnp.int32, (QKV_SUB, LANES), 1)
    first_half = (lane % (2 * ROPE_NFREQ)) < ROPE_NFREQ
    for r0 in range(0, x_ref.shape[0], QKV_SUB):
        r = slice(r0, r0 + QKV_SUB)
        h = _rms(x_ref[r, :], pre_ref[...]).astype(BF16)
        z = _dot(h, w_ref[...])
        cos_t = cos_ref[r, :]
        sin_t = sin_ref[r, :]
        qn = _head_norm(z[:, :ATTN_WIDTH], onesq_ref[...], qg_ref[...])
        for j in range(N_SLABS):
            q_ref[j, r, :] = _rope(qn[:, j * LANES:(j + 1) * LANES], cos_t, sin_t,
                                   first_half).astype(BF16)
        kn = _head_norm(z[:, ATTN_WIDTH:ATTN_WIDTH + KV_WIDTH], onesk_ref[...], kg_ref[...])
        kr = _rope(kn, cos_t, sin_t, first_half)
        kt = kr.T.astype(BF16)
        kt_ref[0, :, r] = kt[:HEAD_DIM]
        kt_ref[1, :, r] = kt[HEAD_DIM:]
        v_ref[r, :] = z[:, ATTN_WIDTH + KV_WIDTH:].astype(BF16)


def _qkv(x3, pre, w_qkv, qg, kg, ones_q, ones_k, cos_t, sin_t):
    b, s, _ = x3.shape
    tm = QKV_TM
    return pl.pallas_call(
        _qkv_kernel,
        out_shape=(
            jax.ShapeDtypeStruct((b, N_SLABS, s, LANES), BF16),
            jax.ShapeDtypeStruct((b, N_KV_HEADS, HEAD_DIM, s), BF16),
            jax.ShapeDtypeStruct((b, s, KV_WIDTH), BF16),
        ),
        grid=(b, s // tm),
        in_specs=[
            pl.BlockSpec((None, tm, D_MODEL), lambda bi, i: (bi, i, 0)),
            _const_spec((1, D_MODEL)),
            _const_spec((D_MODEL, QKV_WIDTH)),
            _const_spec((1, ATTN_WIDTH)),
            _const_spec((1, KV_WIDTH)),
            _const_spec((ATTN_WIDTH, ATTN_WIDTH)),
            _const_spec((KV_WIDTH, KV_WIDTH)),
            pl.BlockSpec((tm, LANES), lambda bi, i: (i, 0)),
            pl.BlockSpec((tm, LANES), lambda bi, i: (i, 0)),
        ],
        out_specs=(
            pl.BlockSpec((None, N_SLABS, tm, LANES), lambda bi, i: (bi, 0, i, 0)),
            pl.BlockSpec((None, N_KV_HEADS, HEAD_DIM, tm), lambda bi, i: (bi, 0, 0, i)),
            pl.BlockSpec((None, tm, KV_WIDTH), lambda bi, i: (bi, i, 0)),
        ),
        compiler_params=pltpu.CompilerParams(
            dimension_semantics=("arbitrary", "arbitrary"), vmem_limit_bytes=VMEM_LIMIT),
        name="qkv",
    )(x3, pre, w_qkv, qg, kg, ones_q, ones_k, cos_t, sin_t)


def _memkv_kernel(m_ref, g_ref, w_ref, k_ref, v_ref):
    mn = _rms(m_ref[...], g_ref[...]).astype(BF16)
    kv = _dot(mn, w_ref[...])
    k_ref[...] = kv[:, :MEM_WIDTH].astype(BF16)
    v_ref[...] = kv[:, MEM_WIDTH:].astype(BF16)


def _memkv(mem, g, w):
    b = mem.shape[0]
    return pl.pallas_call(
        _memkv_kernel,
        out_shape=(
            jax.ShapeDtypeStruct((b, MEM_LEN, MEM_WIDTH), BF16),
            jax.ShapeDtypeStruct((b, MEM_LEN, MEM_WIDTH), BF16),
        ),
        grid=(b,),
        in_specs=[
            pl.BlockSpec((None, MEM_LEN, D_MODEL), lambda bi: (bi, 0, 0)),
            _const_spec((1, D_MODEL)),
            _const_spec((D_MODEL, 2 * MEM_WIDTH)),
        ],
        out_specs=(
            pl.BlockSpec((None, MEM_LEN, MEM_WIDTH), lambda bi: (bi, 0, 0)),
            pl.BlockSpec((None, MEM_LEN, MEM_WIDTH), lambda bi: (bi, 0, 0)),
        ),
        compiler_params=pltpu.CompilerParams(dimension_semantics=("arbitrary",)),
        name="memkv",
    )(mem, g, w)


def _attn_kernel(q_ref, kt_ref, v_ref, o_ref):
    tq = q_ref.shape[1]
    seq = v_ref.shape[0]
    lane = lax.broadcasted_iota(jnp.int32, (tq, LANES), 1)
    lo_half = lane < HEAD_DIM
    ones = jnp.ones((ATT_KC, LANES), BF16)
    grp = pl.program_id(1)
    for j in range(q_ref.shape[0]):
        q = q_ref[j]
        zero = jnp.zeros_like(q)
        halves = []
        for par in range(2):
            lhs = jnp.where(lo_half, q, zero) if par == 0 else jnp.where(lo_half, zero, q)
            m = acc = None
            for c in range(seq // ATT_KC):
                ksl = slice(c * ATT_KC, (c + 1) * ATT_KC)
                kt = kt_ref[:, ksl]
                s = _dot(lhs, jnp.concatenate([kt, kt], axis=0))
                v_ext = jnp.concatenate([v_ref[ksl, :], ones], axis=1)
                mc = jnp.max(s, axis=-1, keepdims=True)
                if c == 0:
                    m = mc
                    acc = _dot(jnp.exp2(s - m).astype(BF16), v_ext)
                else:
                    m_new = jnp.maximum(m, mc)
                    acc = jnp.exp2(m - m_new) * acc + _dot(jnp.exp2(s - m_new).astype(BF16), v_ext)
                    m = m_new
            o = acc[:, :LANES] / acc[:, LANES:]
            o_sw = pltpu.roll(o, HEAD_DIM, axis=1)
            halves.append(jnp.where(grp == par, o, o_sw))
        o_ref[j] = jnp.where(lo_half, halves[0], halves[1]).astype(BF16)


def _attn(q4, kt, v):
    b, _, s, _ = q4.shape
    tq = ATT_TQ
    per_grp = N_SLABS // N_KV_HEADS
    return pl.pallas_call(
        _attn_kernel,
        out_shape=jax.ShapeDtypeStruct((b, N_SLABS, s, LANES), BF16),
        grid=(b, N_KV_HEADS, s // tq),
        in_specs=[
            pl.BlockSpec((None, per_grp, tq, LANES), lambda bi, g, i: (bi, g, i, 0)),
            pl.BlockSpec((None, None, HEAD_DIM, s), lambda bi, g, i: (bi, g, 0, 0)),
            pl.BlockSpec((None, s, KV_WIDTH), lambda bi, g, i: (bi, 0, 0)),
        ],
        out_specs=pl.BlockSpec((None, per_grp, tq, LANES), lambda bi, g, i: (bi, g, i, 0)),
        compiler_params=pltpu.CompilerParams(
            dimension_semantics=("arbitrary", "arbitrary", "arbitrary"),
            vmem_limit_bytes=VMEM_LIMIT),
        name="attn",
    )(q4, kt, v)


def _mixer_kernel(n_cast, x_ref, a_ref, pre_ref, wra_ref, wrb_ref, wbg_ref, bbg_ref, gvn_ref,
                  ws_ref, bs_ref, km_ref, vm_ref, wpa_ref, wpg_ref, wpm_ref, wo_ref, post_ref, *refs):
    o_ref = refs[n_cast]
    for w_ref, wb_ref in zip(refs[:n_cast], refs[n_cast + 1:]):
        wb_ref[...] = w_ref[...].astype(BF16)
    nch = MIX_SUB // GMLP_CHUNK
    gd = GMLP_WIDTH // GMLP_GROUPS
    for r0 in range(0, x_ref.shape[0], MIX_SUB):
        r = slice(r0, r0 + MIX_SUB)
        x = x_ref[r, :]
        hb = _rms(x, pre_ref[...]).astype(BF16)
        z = jnp.concatenate([_dot(hb, wra_ref[...]), _dot(hb, wrb_ref[...])], axis=1)
        gu = jax.nn.gelu(z[:, :GMLP_WIDTH])
        gv = jax.nn.gelu(z[:, GMLP_WIDTH:2 * GMLP_WIDTH])
        qm = z[:, 2 * GMLP_WIDTH:].astype(BF16)

        vn = _rms(gv, gvn_ref[...]).astype(BF16)
        cols = []
        for g in range(GMLP_GROUPS):
            rhs = jnp.concatenate(
                [vn[c * GMLP_CHUNK:(c + 1) * GMLP_CHUNK, g * gd:(g + 1) * gd] for c in range(nch)],
                axis=1)
            mixed = _dot(ws_ref[g], rhs) + jnp.concatenate([bs_ref[g]] * nch, axis=1)
            cols.append(jnp.concatenate(
                [mixed[:, c * gd:(c + 1) * gd] for c in range(nch)], axis=0))
        gm = (gu * jnp.concatenate(cols, axis=1)).astype(BF16)

        outs = []
        for hh in range(MEM_HEADS):
            sl = slice(hh * MEM_HEAD_DIM, (hh + 1) * MEM_HEAD_DIM)
            s = lax.dot_general(qm[:, sl], km_ref[:, sl], (((1,), (1,)), ((), ())),
                                preferred_element_type=F32) * (MEM_HEAD_DIM ** -0.5)
            m = jnp.max(s, axis=-1, keepdims=True)
            p = jnp.exp(s - m)
            l = jnp.sum(p, axis=-1, keepdims=True)
            outs.append(_dot(p.astype(BF16), vm_ref[:, sl]) / l)
        mo = jnp.concatenate(outs, axis=1).astype(BF16)

        att = jnp.concatenate([a_ref[j, r, :] for j in range(N_SLABS)], axis=1)
        gates = jax.nn.sigmoid(_dot(hb, wbg_ref[...]) + bbg_ref[...])
        merged = (gates[:, :D_MODEL] * _dot(att, wpa_ref[...])
                  + gates[:, D_MODEL:2 * D_MODEL] * _dot(gm, wpg_ref[...])
                  + gates[:, 2 * D_MODEL:] * _dot(mo, wpm_ref[...]))
        out = _dot(merged.astype(BF16), wo_ref[...])
        o_ref[r, :] = x + _rms(out, post_ref[...])


def _mixer(x3, a4, pre, w_in_b, w_bg, b_bg, gvn, ws, bs, km, vm, wpa, wpg, wpm, wo, post,
           cast_layer, cast_weights):
    b, s, _ = x3.shape
    tm = MIX_TM
    n_tiles = s // tm
    gd = GMLP_WIDTH // GMLP_GROUPS
    c_in, c_out, c_shapes = _cast_specs(cast_weights, cast_layer, b * n_tiles,
                                        lambda bi, i: bi * n_tiles + i)
    w_in_block = lambda j: pl.BlockSpec((D_MODEL, QKV_WIDTH), lambda bi, i: (0, j),
                                        pipeline_mode=pl.Buffered(1))
    outs = pl.pallas_call(
        functools.partial(_mixer_kernel, len(cast_weights)),
        out_shape=[jax.ShapeDtypeStruct((b, s, D_MODEL), F32)] + c_shapes,
        grid=(b, n_tiles),
        in_specs=[
            pl.BlockSpec((None, tm, D_MODEL), lambda bi, i: (bi, i, 0)),
            pl.BlockSpec((None, N_SLABS, tm, LANES), lambda bi, i: (bi, 0, i, 0)),
            _const_spec((1, D_MODEL)),
            w_in_block(1),
            w_in_block(2),
            _const_spec((D_MODEL, 3 * D_MODEL)),
            _const_spec((1, 3 * D_MODEL)),
            _const_spec((1, GMLP_WIDTH)),
            _const_spec((GMLP_GROUPS, GMLP_CHUNK, GMLP_CHUNK)),
            _const_spec((GMLP_GROUPS, GMLP_CHUNK, gd)),
            pl.BlockSpec((None, MEM_LEN, MEM_WIDTH), lambda bi, i: (bi, 0, 0)),
            pl.BlockSpec((None, MEM_LEN, MEM_WIDTH), lambda bi, i: (bi, 0, 0)),
            _const_spec((ATTN_WIDTH, D_MODEL)),
            _const_spec((GMLP_WIDTH, D_MODEL)),
            _const_spec((MEM_WIDTH, D_MODEL)),
            _const_spec((D_MODEL, D_MODEL)),
            _const_spec((1, D_MODEL)),
        ] + c_in,
        out_specs=[pl.BlockSpec((None, tm, D_MODEL), lambda bi, i: (bi, i, 0))] + c_out,
        compiler_params=pltpu.CompilerParams(
            dimension_semantics=("arbitrary", "arbitrary"), vmem_limit_bytes=VMEM_LIMIT),
        name="mixer",
    )(x3, a4, pre, w_in_b, w_in_b, w_bg, b_bg, gvn, ws, bs, km, vm, wpa, wpg, wpm, wo, post,
      *cast_weights)
    return outs[0], outs[1:]


def _rope_tables(seq):
    f32 = np.float32
    rows = seq // GRID_W
    row = np.repeat(np.arange(rows, dtype=f32), GRID_W)
    col = np.tile(np.arange(GRID_W, dtype=f32), rows)
    inv_freq = (f32(ROPE_THETA) ** (-np.arange(ROPE_NFREQ, dtype=f32) / f32(ROPE_NFREQ))).astype(f32)
    ang = np.stack([row[:, None] * inv_freq, col[:, None] * inv_freq], axis=1)
    cos, sin = np.cos(ang).astype(f32), np.sin(ang).astype(f32)
    cos64 = np.concatenate([cos[:, 0], cos[:, 0], cos[:, 1], cos[:, 1]], axis=1)
    sin64 = np.concatenate([-sin[:, 0], sin[:, 0], -sin[:, 1], sin[:, 1]], axis=1)
    reps = LANES // HEAD_DIM
    return jnp.asarray(np.tile(cos64, (1, reps))), jnp.asarray(np.tile(sin64, (1, reps)))


def _block_ones(width):
    idx = np.arange(width) // HEAD_DIM
    return jnp.asarray((idx[:, None] == idx[None, :]).astype(np.float32), dtype=BF16)


def kernel(x, mem, ffn1_pre, ffn1_w_gate, ffn1_w_up, ffn1_w_down, ffn1_post, mix_pre, mem_norm, w_in, w_mem_kv, q_norm, k_norm, gmlp_v_norm, gmlp_w_s, gmlp_b_s, w_branch_gate, b_branch_gate, w_proj_attn, w_proj_gmlp, w_proj_mem, w_out, mix_post, ffn2_pre, ffn2_w_gate, ffn2_w_up, ffn2_w_down, ffn2_post):
    b, s, d = x.shape
    depth = w_in.shape[0]
    cos_t, sin_t = _rope_tables(s)
    ones_q = _block_ones(ATTN_WIDTH)
    ones_k = _block_ones(KV_WIDTH)
    gd = GMLP_WIDTH // GMLP_GROUPS
    row = lambda v: v.reshape(1, -1)
    bf = lambda w: w.astype(BF16)

    for l in range(depth):
        x, (w_in_b, w_mkv_b, w_bg_b, wpa_b, wpg_b, wpm_b, wo_b) = _ffn(
            x.reshape(b * s, d), row(ffn1_pre[l]), ffn1_w_gate, ffn1_w_up, ffn1_w_down,
            row(ffn1_post[l]), own_layer=l, cast_layer=l,
            cast_weights=(w_in, w_mem_kv, w_branch_gate, w_proj_attn, w_proj_gmlp, w_proj_mem,
                          w_out))
        x = x.reshape(b, s, d)

        qg = row(jnp.tile(q_norm[l], N_Q_HEADS)) * (HEAD_DIM ** -0.5 * np.log2(np.e))
        kg = row(jnp.tile(k_norm[l], N_KV_HEADS))
        q4, kt, v = _qkv(x, row(mix_pre[l]), w_in_b, qg, kg, ones_q, ones_k, cos_t, sin_t)
        km, vm = _memkv(mem, row(mem_norm[l]), w_mkv_b)
        a4 = _attn(q4, kt, v)
        bs = jnp.broadcast_to(gmlp_b_s[l][:, :, None], (GMLP_GROUPS, GMLP_CHUNK, gd))
        x, (wg2_b, wu2_b, wd2_b) = _mixer(
            x, a4, row(mix_pre[l]), w_in_b, w_bg_b, row(b_branch_gate[l]), row(gmlp_v_norm[l]),
            bf(gmlp_w_s[l]), bs, km, vm, wpa_b, wpg_b, wpm_b, wo_b, row(mix_post[l]),
            cast_layer=l, cast_weights=(ffn2_w_gate, ffn2_w_up, ffn2_w_down))

        x, _ = _ffn(x.reshape(b * s, d), row(ffn2_pre[l]), wg2_b, wu2_b, wd2_b, row(ffn2_post[l]))
        x = x.reshape(b, s, d)
    return x
```

```python
import functools

import numpy as np
import jax
import jax.numpy as jnp
from jax import lax
from jax.experimental import pallas as pl
from jax.experimental.pallas import tpu as pltpu

D_MODEL = 1024
MEM_LEN = 256
GRID_W = 64
EPS = 1e-6
HEAD_DIM = 64
ATTN_WIDTH = 512
N_Q_HEADS = 8
N_KV_HEADS = 2
KV_WIDTH = 128
QKV_WIDTH = ATTN_WIDTH + 2 * KV_WIDTH
ROPE_THETA = 10000.0
ROPE_NFREQ = 16
GMLP_WIDTH = 512
GMLP_GROUPS = 4
GMLP_CHUNK = 128
MEM_HEADS = 4
MEM_HEAD_DIM = 128
MEM_WIDTH = 512
D_FF = 2816

LANES = 128
BF16_SUBLANES = 16
N_SLABS = ATTN_WIDTH // LANES
VMEM_LIMIT = 56 * 1024 * 1024

FFN_TM = 1024
FFN_SUB = 512
FFN_FC = 256
FFN_STAGE_ROWS = 64
FFN_STAGE_SLOTS = 3
QKV_TM = 2048
QKV_SUB = 512
ATT_TQ = 2048
ATT_KC = 256
MIX_TM = 1024
MIX_SUB = 512

BF16 = jnp.bfloat16
F32 = jnp.float32


def _rms(x, g):
    ms = jnp.mean(x * x, axis=-1, keepdims=True)
    return x * lax.rsqrt(ms + EPS) * g


def _dot(a, b):
    return jnp.dot(a, b, preferred_element_type=F32)


def _const_spec(shape):
    nd = len(shape)
    return pl.BlockSpec(shape, lambda *_: (0,) * nd, pipeline_mode=pl.Buffered(1))


def _cast_block_rows(n_rows, n_steps):
    for rows in range(BF16_SUBLANES, n_rows + 1, BF16_SUBLANES):
        if n_rows % rows == 0 and n_rows // rows <= n_steps:
            return rows
    raise ValueError((n_rows, n_steps))


def _cast_specs(weights, layer, n_grid, step_of):
    in_specs, out_specs, out_shapes = [], [], []
    for w in weights:
        _, n_rows, n_cols = w.shape
        rows = _cast_block_rows(n_rows, n_grid)
        last = n_rows // rows - 1

        def blk(*ids, last=last):
            return jnp.minimum(step_of(*ids), last)

        in_specs.append(pl.BlockSpec((None, rows, n_cols), lambda *ids, blk=blk: (layer, blk(*ids), 0)))
        out_specs.append(pl.BlockSpec((rows, n_cols), lambda *ids, blk=blk: (blk(*ids), 0)))
        out_shapes.append(jax.ShapeDtypeStruct((n_rows, n_cols), BF16))
    return in_specs, out_specs, out_shapes


def _stage_copy(src_hbm, stg_ref, sem, job, c):
    slots, rows = stg_ref.shape[0], stg_ref.shape[1]
    return pltpu.make_async_copy(src_hbm.at[pl.ds(c * rows, rows), :], stg_ref.at[c % slots],
                                 sem.at[job, c % slots])


def _stage_weights(jobs, sem):
    n_chunks = [src.shape[0] // stg.shape[1] for src, _, stg in jobs]
    for k, (src, _, stg) in enumerate(jobs):
        for c in range(min(stg.shape[0], n_chunks[k])):
            _stage_copy(src, stg, sem, k, c).start()
    for c in range(max(n_chunks)):
        for k, (src, dst, stg) in enumerate(jobs):
            if c >= n_chunks[k]:
                continue
            slots, rows = stg.shape[0], stg.shape[1]
            _stage_copy(src, stg, sem, k, c).wait()
            dst[c * rows:(c + 1) * rows, :] = stg[c % slots].astype(BF16)
            if c + slots < n_chunks[k]:
                _stage_copy(src, stg, sem, k, c + slots).start()


def _ffn_kernel(own_layer, n_cast, x_ref, pre_ref, wg_in, wu_in, wd_in, post_ref, *refs):
    o_ref = refs[n_cast]
    if own_layer is None:
        wg_ref, wu_ref, wd_ref = wg_in, wu_in, wd_in
    else:
        wg_ref, wu_ref, wd_ref, stg_a, stg_b, stg_d, sem = refs[2 * n_cast + 1:]

        @pl.when(pl.program_id(0) == 0)
        def _():
            _stage_weights([(wg_in.at[own_layer], wg_ref, stg_a),
                            (wu_in.at[own_layer], wu_ref, stg_b),
                            (wd_in.at[own_layer], wd_ref, stg_d)], sem)

    for w_ref, wb_ref in zip(refs[:n_cast], refs[n_cast + 1:2 * n_cast + 1]):
        wb_ref[...] = w_ref[...].astype(BF16)

    n_sub = x_ref.shape[0] // FFN_SUB
    rows = [slice(r * FFN_SUB, (r + 1) * FFN_SUB) for r in range(n_sub)]
    hs = [_rms(x_ref[r, :], pre_ref[...]).astype(BF16) for r in rows]
    accs = [None] * n_sub
    for c in range(D_FF // FFN_FC):
        sl = slice(c * FFN_FC, (c + 1) * FFN_FC)
        for k in range(n_sub):
            g = _dot(hs[k], wg_ref[:, sl])
            u = _dot(hs[k], wu_ref[:, sl])
            a = (g * jax.nn.sigmoid(g) * u).astype(BF16)
            d = _dot(a, wd_ref[sl, :])
            accs[k] = d if c == 0 else accs[k] + d
    for k, r in enumerate(rows):
        o_ref[r, :] = x_ref[r, :] + 0.5 * _rms(accs[k], post_ref[...])


def _ffn(x2d, pre, wg, wu, wd, post, own_layer=None, cast_layer=0, cast_weights=()):
    t = x2d.shape[0]
    tm = FFN_TM
    n_grid = t // tm
    c_in, c_out, c_shapes = _cast_specs(cast_weights, cast_layer, n_grid, lambda i: i)
    if own_layer is None:
        w_specs = [_const_spec((D_MODEL, D_FF)), _const_spec((D_MODEL, D_FF)),
                   _const_spec((D_FF, D_MODEL))]
        scratch = []
    else:
        w_specs = [pl.BlockSpec(memory_space=pl.ANY)] * 3
        scratch = [
            pltpu.VMEM((D_MODEL, D_FF), BF16),
            pltpu.VMEM((D_MODEL, D_FF), BF16),
            pltpu.VMEM((D_FF, D_MODEL), BF16),
            pltpu.VMEM((FFN_STAGE_SLOTS, FFN_STAGE_ROWS, D_FF), F32),
            pltpu.VMEM((FFN_STAGE_SLOTS, FFN_STAGE_ROWS, D_FF), F32),
            pltpu.VMEM((FFN_STAGE_SLOTS, FFN_STAGE_ROWS * D_FF // D_MODEL, D_MODEL), F32),
            pltpu.SemaphoreType.DMA((3, FFN_STAGE_SLOTS)),
        ]
    outs = pl.pallas_call(
        functools.partial(_ffn_kernel, own_layer, len(cast_weights)),
        out_shape=[jax.ShapeDtypeStruct((t, D_MODEL), F32)] + c_shapes,
        grid=(n_grid,),
        in_specs=[pl.BlockSpec((tm, D_MODEL), lambda i: (i, 0)), _const_spec((1, D_MODEL))]
        + w_specs + [_const_spec((1, D_MODEL))] + c_in,
        out_specs=[pl.BlockSpec((tm, D_MODEL), lambda i: (i, 0))] + c_out,
        scratch_shapes=scratch,
        compiler_params=pltpu.CompilerParams(
            dimension_semantics=("arbitrary",), vmem_limit_bytes=VMEM_LIMIT),
        name="ffn",
    )(x2d, pre, wg, wu, wd, post, *cast_weights)
    return outs[0], outs[1:]


def _head_norm(z, ones_bd, gain):
    z2 = z * z
    hi = z2.astype(BF16)
    lo = (z2 - hi.astype(F32)).astype(BF16)
    ssum = _dot(hi, ones_bd) + _dot(lo, ones_bd)
    return z * lax.rsqrt(ssum * (1.0 / HEAD_DIM) + EPS) * gain


def _rope(x, cos_t, sin_t, first_half):
    up = pltpu.roll(x, LANES - ROPE_NFREQ, axis=1)
    dn = pltpu.roll(x, ROPE_NFREQ, axis=1)
    return x * cos_t + jnp.where(first_half, up, dn) * sin_t


def _qkv_kernel(x_ref, pre_ref, w_ref, qg_ref, kg_ref, onesq_ref, onesk_ref,
                cos_ref, sin_ref, q_ref, kt_ref, v_ref):
    lane = lax.broadcasted_iota(jnp.int32, (QKV_SUB, LANES), 1)
    first_half = (lane % (2 * ROPE_NFREQ)) < ROPE_NFREQ
    for r0 in range(0, x_ref.shape[0], QKV_SUB):
        r = slice(r0, r0 + QKV_SUB)
        h = _rms(x_ref[r, :], pre_ref[...]).astype(BF16)
        z = _dot(h, w_ref[...])
        cos_t = cos_ref[r, :]
        sin_t = sin_ref[r, :]
        qn = _head_norm(z[:, :ATTN_WIDTH], onesq_ref[...], qg_ref[...])
        for j in range(N_SLABS):
            q_ref[j, r, :] = _rope(qn[:, j * LANES:(j + 1) * LANES], cos_t, sin_t,
                                   first_half).astype(BF16)
        kn = _head_norm(z[:, ATTN_WIDTH:ATTN_WIDTH + KV_WIDTH], onesk_ref[...], kg_ref[...])
        kr = _rope(kn, cos_t, sin_t, first_half)
        kt = kr.T.astype(BF16)
        kt_ref[0, :, r] = kt[:HEAD_DIM]
        kt_ref[1, :, r] = kt[HEAD_DIM:]
        v_ref[r, :] = z[:, ATTN_WIDTH + KV_WIDTH:].astype(BF16)


def _qkv(x3, pre, w_qkv, qg, kg, ones_q, ones_k, cos_t, sin_t):
    b, s, _ = x3.shape
    tm = QKV_TM
    return pl.pallas_call(
        _qkv_kernel,
        out_shape=(
            jax.ShapeDtypeStruct((b, N_SLABS, s, LANES), BF16),
            jax.ShapeDtypeStruct((b, N_KV_HEADS, HEAD_DIM, s), BF16),
            jax.ShapeDtypeStruct((b, s, KV_WIDTH), BF16),
        ),
        grid=(b, s // tm),
        in_specs=[
            pl.BlockSpec((None, tm, D_MODEL), lambda bi, i: (bi, i, 0)),
            _const_spec((1, D_MODEL)),
            _const_spec((D_MODEL, QKV_WIDTH)),
            _const_spec((1, ATTN_WIDTH)),
            _const_spec((1, KV_WIDTH)),
            _const_spec((ATTN_WIDTH, ATTN_WIDTH)),
            _const_spec((KV_WIDTH, KV_WIDTH)),
            pl.BlockSpec((tm, LANES), lambda bi, i: (i, 0)),
            pl.BlockSpec((tm, LANES), lambda bi, i: (i, 0)),
        ],
        out_specs=(
            pl.BlockSpec((None, N_SLABS, tm, LANES), lambda bi, i: (bi, 0, i, 0)),
            pl.BlockSpec((None, N_KV_HEADS, HEAD_DIM, tm), lambda bi, i: (bi, 0, 0, i)),
            pl.BlockSpec((None, tm, KV_WIDTH), lambda bi, i: (bi, i, 0)),
        ),
        compiler_params=pltpu.CompilerParams(
            dimension_semantics=("arbitrary", "arbitrary"), vmem_limit_bytes=VMEM_LIMIT),
        name="qkv",
    )(x3, pre, w_qkv, qg, kg, ones_q, ones_k, cos_t, sin_t)


def _memkv_kernel(m_ref, g_ref, w_ref, k_ref, v_ref):
    mn = _rms(m_ref[...], g_ref[...]).astype(BF16)
    kv = _dot(mn, w_ref[...])
    k_ref[...] = kv[:, :MEM_WIDTH].astype(BF16)
    v_ref[...] = kv[:, MEM_WIDTH:].astype(BF16)


def _memkv(mem, g, w):
    b = mem.shape[0]
    return pl.pallas_call(
        _memkv_kernel,
        out_shape=(
            jax.ShapeDtypeStruct((b, MEM_LEN, MEM_WIDTH), BF16),
            jax.ShapeDtypeStruct((b, MEM_LEN, MEM_WIDTH), BF16),
        ),
        grid=(b,),
        in_specs=[
            pl.BlockSpec((None, MEM_LEN, D_MODEL), lambda bi: (bi, 0, 0)),
            _const_spec((1, D_MODEL)),
            _const_spec((D_MODEL, 2 * MEM_WIDTH)),
        ],
        out_specs=(
            pl.BlockSpec((None, MEM_LEN, MEM_WIDTH), lambda bi: (bi, 0, 0)),
            pl.BlockSpec((None, MEM_LEN, MEM_WIDTH), lambda bi: (bi, 0, 0)),
        ),
        compiler_params=pltpu.CompilerParams(dimension_semantics=("arbitrary",)),
        name="memkv",
    )(mem, g, w)


def _attn_kernel(q_ref, kt_ref, v_ref, o_ref):
    tq = q_ref.shape[1]
    seq = v_ref.shape[0]
    lane = lax.broadcasted_iota(jnp.int32, (tq, LANES), 1)
    lo_half = lane < HEAD_DIM
    ones = jnp.ones((ATT_KC, LANES), BF16)
    grp = pl.program_id(1)
    for j in range(q_ref.shape[0]):
        q = q_ref[j]
        zero = jnp.zeros_like(q)
        halves = []
        for par in range(2):
            lhs = jnp.where(lo_half, q, zero) if par == 0 else jnp.where(lo_half, zero, q)
            m = acc = None
            for c in range(seq // ATT_KC):
                ksl = slice(c * ATT_KC, (c + 1) * ATT_KC)
                kt = kt_ref[:, ksl]
                s = _dot(lhs, jnp.concatenate([kt, kt], axis=0))
                v_ext = jnp.concatenate([v_ref[ksl, :], ones], axis=1)
                mc = jnp.max(s, axis=-1, keepdims=True)
                if c == 0:
                    m = mc
                    acc = _dot(jnp.exp2(s - m).astype(BF16), v_ext)
                else:
                    m_new = jnp.maximum(m, mc)
                    acc = jnp.exp2(m - m_new) * acc + _dot(jnp.exp2(s - m_new).astype(BF16), v_ext)
                    m = m_new
            o = acc[:, :LANES] / acc[:, LANES:]
            o_sw = pltpu.roll(o, HEAD_DIM, axis=1)
            halves.append(jnp.where(grp == par, o, o_sw))
        o_ref[j] = jnp.where(lo_half, halves[0], halves[1]).astype(BF16)


def _attn(q4, kt, v):
    b, _, s, _ = q4.shape
    tq = ATT_TQ
    per_grp = N_SLABS // N_KV_HEADS
    return pl.pallas_call(
        _attn_kernel,
        out_shape=jax.ShapeDtypeStruct((b, N_SLABS, s, LANES), BF16),
        grid=(b, N_KV_HEADS, s // tq),
        in_specs=[
            pl.BlockSpec((None, per_grp, tq, LANES), lambda bi, g, i: (bi, g, i, 0)),
            pl.BlockSpec((None, None, HEAD_DIM, s), lambda bi, g, i: (bi, g, 0, 0)),
            pl.BlockSpec((None, s, KV_WIDTH), lambda bi, g, i: (bi, 0, 0)),
        ],
        out_specs=pl.BlockSpec((None, per_grp, tq, LANES), lambda bi, g, i: (bi, g, i, 0)),
        compiler_params=pltpu.CompilerParams(
            dimension_semantics=("arbitrary", "arbitrary", "arbitrary"),
            vmem_limit_bytes=VMEM_LIMIT),
        name="attn",
    )(q4, kt, v)


def _mixer_kernel(n_cast, x_ref, a_ref, pre_ref, wra_ref, wrb_ref, wbg_ref, bbg_ref, gvn_ref,
                  ws_ref, bs_ref, km_ref, vm_ref, wpa_ref, wpg_ref, wpm_ref, wo_ref, post_ref, *refs):
    o_ref = refs[n_cast]
    for w_ref, wb_ref in zip(refs[:n_cast], refs[n_cast + 1:]):
        wb_ref[...] = w_ref[...].astype(BF16)
    nch = MIX_SUB // GMLP_CHUNK
    gd = GMLP_WIDTH // GMLP_GROUPS
    for r0 in range(0, x_ref.shape[0], MIX_SUB):
        r = slice(r0, r0 + MIX_SUB)
        x = x_ref[r, :]
        hb = _rms(x, pre_ref[...]).astype(BF16)
        z = jnp.concatenate([_dot(hb, wra_ref[...]), _dot(hb, wrb_ref[...])], axis=1)
        gu = jax.nn.gelu(z[:, :GMLP_WIDTH])
        gv = jax.nn.gelu(z[:, GMLP_WIDTH:2 * GMLP_WIDTH])
        qm = z[:, 2 * GMLP_WIDTH:].astype(BF16)

        vn = _rms(gv, gvn_ref[...]).astype(BF16)
        cols = []
        for g in range(GMLP_GROUPS):
            rhs = jnp.concatenate(
                [vn[c * GMLP_CHUNK:(c + 1) * GMLP_CHUNK, g * gd:(g + 1) * gd] for c in range(nch)],
                axis=1)
            mixed = _dot(ws_ref[g], rhs) + jnp.concatenate([bs_ref[g]] * nch, axis=1)
            cols.append(jnp.concatenate(
                [mixed[:, c * gd:(c + 1) * gd] for c in range(nch)], axis=0))
        gm = (gu * jnp.concatenate(cols, axis=1)).astype(BF16)

        outs = []
        for hh in range(MEM_HEADS):
            sl = slice(hh * MEM_HEAD_DIM, (hh + 1) * MEM_HEAD_DIM)
            s = lax.dot_general(qm[:, sl], km_ref[:, sl], (((1,), (1,)), ((), ())),
                                preferred_element_type=F32) * (MEM_HEAD_DIM ** -0.5)
            m = jnp.max(s, axis=-1, keepdims=True)
            p = jnp.exp(s - m)
            l = jnp.sum(p, axis=-1, keepdims=True)
            outs.append(_dot(p.astype(BF16), vm_ref[:, sl]) / l)
        mo = jnp.concatenate(outs, axis=1).astype(BF16)

        att = jnp.concatenate([a_ref[j, r, :] for j in range(N_SLABS)], axis=1)
        gates = jax.nn.sigmoid(_dot(hb, wbg_ref[...]) + bbg_ref[...])
        merged = (gates[:, :D_MODEL] * _dot(att, wpa_ref[...])
                  + gates[:, D_MODEL:2 * D_MODEL] * _dot(gm, wpg_ref[...])
                  + gates[:, 2 * D_MODEL:] * _dot(mo, wpm_ref[...]))
        out = _dot(merged.astype(BF16), wo_ref[...])
        o_ref[r, :] = x + _rms(out, post_ref[...])


def _mixer(x3, a4, pre, w_in_b, w_bg, b_bg, gvn, ws, bs, km, vm, wpa, wpg, wpm, wo, post,
           cast_layer, cast_weights):
    b, s, _ = x3.shape
    tm = MIX_TM
    n_tiles = s // tm
    gd = GMLP_WIDTH // GMLP_GROUPS
    c_in, c_out, c_shapes = _cast_specs(cast_weights, cast_layer, b * n_tiles,
                                        lambda bi, i: bi * n_tiles + i)
    w_in_block = lambda j: pl.BlockSpec((D_MODEL, QKV_WIDTH), lambda bi, i: (0, j),
                                        pipeline_mode=pl.Buffered(1))
    outs = pl.pallas_call(
        functools.partial(_mixer_kernel, len(cast_weights)),
        out_shape=[jax.ShapeDtypeStruct((b, s, D_MODEL), F32)] + c_shapes,
        grid=(b, n_tiles),
        in_specs=[
            pl.BlockSpec((None, tm, D_MODEL), lambda bi, i: (bi, i, 0)),
            pl.BlockSpec((None, N_SLABS, tm, LANES), lambda bi, i: (bi, 0, i, 0)),
            _const_spec((1, D_MODEL)),
            w_in_block(1),
            w_in_block(2),
            _const_spec((D_MODEL, 3 * D_MODEL)),
            _const_spec((1, 3 * D_MODEL)),
            _const_spec((1, GMLP_WIDTH)),
            _const_spec((GMLP_GROUPS, GMLP_CHUNK, GMLP_CHUNK)),
            _const_spec((GMLP_GROUPS, GMLP_CHUNK, gd)),
            pl.BlockSpec((None, MEM_LEN, MEM_WIDTH), lambda bi, i: (bi, 0, 0)),
            pl.BlockSpec((None, MEM_LEN, MEM_WIDTH), lambda bi, i: (bi, 0, 0)),
            _const_spec((ATTN_WIDTH, D_MODEL)),
            _const_spec((GMLP_WIDTH, D_MODEL)),
            _const_spec((MEM_WIDTH, D_MODEL)),
            _const_spec((D_MODEL, D_MODEL)),
            _const_spec((1, D_MODEL)),
        ] + c_in,
        out_specs=[pl.BlockSpec((None, tm, D_MODEL), lambda bi, i: (bi, i, 0))] + c_out,
        compiler_params=pltpu.CompilerParams(
            dimension_semantics=("arbitrary", "arbitrary"), vmem_limit_bytes=VMEM_LIMIT),
        name="mixer",
    )(x3, a4, pre, w_in_b, w_in_b, w_bg, b_bg, gvn, ws, bs, km, vm, wpa, wpg, wpm, wo, post,
      *cast_weights)
    return outs[0], outs[1:]


def _rope_tables(seq):
    f32 = np.float32
    rows = seq // GRID_W
    row = np.repeat(np.arange(rows, dtype=f32), GRID_W)
    col = np.tile(np.arange(GRID_W, dtype=f32), rows)
    inv_freq = (f32(ROPE_THETA) ** (-np.arange(ROPE_NFREQ, dtype=f32) / f32(ROPE_NFREQ))).astype(f32)
    ang = np.stack([row[:, None] * inv_freq, col[:, None] * inv_freq], axis=1)
    cos, sin = np.cos(ang).astype(f32), np.sin(ang).astype(f32)
    cos64 = np.concatenate([cos[:, 0], cos[:, 0], cos[:, 1], cos[:, 1]], axis=1)
    sin64 = np.concatenate([-sin[:, 0], sin[:, 0], -sin[:, 1], sin[:, 1]], axis=1)
    reps = LANES // HEAD_DIM
    return jnp.asarray(np.tile(cos64, (1, reps))), jnp.asarray(np.tile(sin64, (1, reps)))


def _block_ones(width):
    idx = np.arange(width) // HEAD_DIM
    return jnp.asarray((idx[:, None] == idx[None, :]).astype(np.float32), dtype=BF16)


def kernel(x, mem, ffn1_pre, ffn1_w_gate, ffn1_w_up, ffn1_w_down, ffn1_post, mix_pre, mem_norm, w_in, w_mem_kv, q_norm, k_norm, gmlp_v_norm, gmlp_w_s, gmlp_b_s, w_branch_gate, b_branch_gate, w_proj_attn, w_proj_gmlp, w_proj_mem, w_out, mix_post, ffn2_pre, ffn2_w_gate, ffn2_w_up, ffn2_w_down, ffn2_post):
    b, s, d = x.shape
    depth = w_in.shape[0]
    cos_t, sin_t = _rope_tables(s)
    ones_q = _block_ones(ATTN_WIDTH)
    ones_k = _block_ones(KV_WIDTH)
    gd = GMLP_WIDTH // GMLP_GROUPS
    row = lambda v: v.reshape(1, -1)
    bf = lambda w: w.astype(BF16)

    for l in range(depth):
        x, (w_in_b, w_mkv_b, w_bg_b, wpa_b, wpg_b, wpm_b, wo_b) = _ffn(
            x.reshape(b * s, d), row(ffn1_pre[l]), ffn1_w_gate, ffn1_w_up, ffn1_w_down,
            row(ffn1_post[l]), own_layer=l, cast_layer=l,
            cast_weights=(w_in, w_mem_kv, w_branch_gate, w_proj_attn, w_proj_gmlp, w_proj_mem,
                          w_out))
        x = x.reshape(b, s, d)

        qg = row(jnp.tile(q_norm[l], N_Q_HEADS)) * (HEAD_DIM ** -0.5 * np.log2(np.e))
        kg = row(jnp.tile(k_norm[l], N_KV_HEADS))
        q4, kt, v = _qkv(x, row(mix_pre[l]), w_in_b, qg, kg, ones_q, ones_k, cos_t, sin_t)
        km, vm = _memkv(mem, row(mem_norm[l]), w_mkv_b)
        a4 = _attn(q4, kt, v)
        bs = jnp.broadcast_to(gmlp_b_s[l][:, :, None], (GMLP_GROUPS, GMLP_CHUNK, gd))
        x, (wg2_b, wu2_b, wd2_b) = _mixer(
            x, a4, row(mix_pre[l]), w_in_b, w_bg_b, row(b_branch_gate[l]), row(gmlp_v_norm[l]),
            bf(gmlp_w_s[l]), bs, km, vm, wpa_b, wpg_b, wpm_b, wo_b, row(mix_post[l]),
            cast_layer=l, cast_weights=(ffn2_w_gate, ffn2_w_up, ffn2_w_down))

        x, _ = _ffn(x.reshape(b * s, d), row(ffn2_pre[l]), wg2_b, wu2_b, wd2_b, row(ffn2_post[l]))
        x = x.reshape(b, s, d)
    return x
```

```python
import functools

import numpy as np
import jax
import jax.numpy as jnp
from jax import lax
from jax.experimental import pallas as pl
from jax.experimental.pallas import tpu as pltpu

D_MODEL = 1024
MEM_LEN = 256
GRID_W = 64
EPS = 1e-6
HEAD_DIM = 64
ATTN_WIDTH = 512
N_Q_HEADS = 8
N_KV_HEADS = 2
KV_WIDTH = 128
QKV_WIDTH = ATTN_WIDTH + 2 * KV_WIDTH
ROPE_THETA = 10000.0
ROPE_NFREQ = 16
GMLP_WIDTH = 512
GMLP_GROUPS = 4
GMLP_CHUNK = 128
MEM_HEADS = 4
MEM_HEAD_DIM = 128
MEM_WIDTH = 512
D_FF = 2816

LANES = 128
BF16_SUBLANES = 16
N_SLABS = ATTN_WIDTH // LANES
VMEM_LIMIT = 56 * 1024 * 1024

FFN_TM = 1024
FFN_SUB = 512
FFN_FC = 256
FFN_STAGE_ROWS = 64
FFN_STAGE_SLOTS = 3
QKV_TM = 2048
QKV_SUB = 512
ATT_TQ = 1024
ATT_KC = 256
MIX_TM = 1024
MIX_SUB = 512

BF16 = jnp.bfloat16
F32 = jnp.float32


def _rms(x, g):
    ms = jnp.mean(x * x, axis=-1, keepdims=True)
    return x * lax.rsqrt(ms + EPS) * g


def _dot(a, b):
    return jnp.dot(a, b, preferred_element_type=F32)


def _const_spec(shape):
    nd = len(shape)
    return pl.BlockSpec(shape, lambda *_: (0,) * nd, pipeline_mode=pl.Buffered(1))


def _cast_block_rows(n_rows, n_steps):
    for rows in range(BF16_SUBLANES, n_rows + 1, BF16_SUBLANES):
        if n_rows % rows == 0 and n_rows // rows <= n_steps:
            return rows
    raise ValueError((n_rows, n_steps))


def _cast_specs(weights, layer, n_grid, step_of):
    in_specs, out_specs, out_shapes = [], [], []
    for w in weights:
        _, n_rows, n_cols = w.shape
        rows = _cast_block_rows(n_rows, n_grid)
        last = n_rows // rows - 1

        def blk(*ids, last=last):
            return jnp.minimum(step_of(*ids), last)

        in_specs.append(pl.BlockSpec((None, rows, n_cols), lambda *ids, blk=blk: (layer, blk(*ids), 0)))
        out_specs.append(pl.BlockSpec((rows, n_cols), lambda *ids, blk=blk: (blk(*ids), 0)))
        out_shapes.append(jax.ShapeDtypeStruct((n_rows, n_cols), BF16))
    return in_specs, out_specs, out_shapes


def _stage_copy(src_hbm, stg_ref, sem, job, c):
    slots, rows = stg_ref.shape[0], stg_ref.shape[1]
    return pltpu.make_async_copy(src_hbm.at[pl.ds(c * rows, rows), :], stg_ref.at[c % slots],
                                 sem.at[job, c % slots])


def _stage_weights(jobs, sem):
    n_chunks = [src.shape[0] // stg.shape[1] for src, _, stg in jobs]
    for k, (src, _, stg) in enumerate(jobs):
        for c in range(min(stg.shape[0], n_chunks[k])):
            _stage_copy(src, stg, sem, k, c).start()
    for c in range(max(n_chunks)):
        for k, (src, dst, stg) in enumerate(jobs):
            if c >= n_chunks[k]:
                continue
            slots, rows = stg.shape[0], stg.shape[1]
            _stage_copy(src, stg, sem, k, c).wait()
            dst[c * rows:(c + 1) * rows, :] = stg[c % slots].astype(BF16)
            if c + slots < n_chunks[k]:
                _stage_copy(src, stg, sem, k, c + slots).start()


def _ffn_kernel(own_layer, n_cast, x_ref, pre_ref, wg_in, wu_in, wd_in, post_ref, *refs):
    o_ref = refs[n_cast]
    if own_layer is None:
        wg_ref, wu_ref, wd_ref = wg_in, wu_in, wd_in
    else:
        wg_ref, wu_ref, wd_ref, stg_a, stg_b, stg_d, sem = refs[2 * n_cast + 1:]

        @pl.when(pl.program_id(0) == 0)
        def _():
            _stage_weights([(wg_in.at[own_layer], wg_ref, stg_a),
                            (wu_in.at[own_layer], wu_ref, stg_b),
                            (wd_in.at[own_layer], wd_ref, stg_d)], sem)

    for w_ref, wb_ref in zip(refs[:n_cast], refs[n_cast + 1:2 * n_cast + 1]):
        wb_ref[...] = w_ref[...].astype(BF16)

    n_sub = x_ref.shape[0] // FFN_SUB
    rows = [slice(r * FFN_SUB, (r + 1) * FFN_SUB) for r in range(n_sub)]
    hs = [_rms(x_ref[r, :], pre_ref[...]).astype(BF16) for r in rows]
    accs = [None] * n_sub
    for c in range(D_FF // FFN_FC):
        sl = slice(c * FFN_FC, (c + 1) * FFN_FC)
        for k in range(n_sub):
            g = _dot(hs[k], wg_ref[:, sl])
            u = _dot(hs[k], wu_ref[:, sl])
            a = (g * jax.nn.sigmoid(g) * u).astype(BF16)
            d = _dot(a, wd_ref[sl, :])
            accs[k] = d if c == 0 else accs[k] + d
    for k, r in enumerate(rows):
        o_ref[r, :] = x_ref[r, :] + 0.5 * _rms(accs[k], post_ref[...])


def _ffn(x2d, pre, wg, wu, wd, post, own_layer=None, cast_layer=0, cast_weights=()):
    t = x2d.shape[0]
    n_grid = t // FFN_TM
    c_in, c_out, c_shapes = _cast_specs(cast_weights, cast_layer, n_grid, lambda i: i)
    if own_layer is None:
        w_specs = [_const_spec((D_MODEL, D_FF)), _const_spec((D_MODEL, D_FF)),
                   _const_spec((D_FF, D_MODEL))]
        scratch = []
    else:
        w_specs = [pl.BlockSpec(memory_space=pl.ANY)] * 3
        scratch = [
            pltpu.VMEM((D_MODEL, D_FF), BF16),
            pltpu.VMEM((D_MODEL, D_FF), BF16),
            pltpu.VMEM((D_FF, D_MODEL), BF16),
            pltpu.VMEM((FFN_STAGE_SLOTS, FFN_STAGE_ROWS, D_FF), F32),
            pltpu.VMEM((FFN_STAGE_SLOTS, FFN_STAGE_ROWS, D_FF), F32),
            pltpu.VMEM((FFN_STAGE_SLOTS, FFN_STAGE_ROWS * D_FF // D_MODEL, D_MODEL), F32),
            pltpu.SemaphoreType.DMA((3, FFN_STAGE_SLOTS)),
        ]
    outs = pl.pallas_call(
        functools.partial(_ffn_kernel, own_layer, len(cast_weights)),
        out_shape=[jax.ShapeDtypeStruct((t, D_MODEL), F32)] + c_shapes,
        grid=(n_grid,),
        in_specs=[pl.BlockSpec((FFN_TM, D_MODEL), lambda i: (i, 0)), _const_spec((1, D_MODEL))]
        + w_specs + [_const_spec((1, D_MODEL))] + c_in,
        out_specs=[pl.BlockSpec((FFN_TM, D_MODEL), lambda i: (i, 0))] + c_out,
        scratch_shapes=scratch,
        compiler_params=pltpu.CompilerParams(
            dimension_semantics=("arbitrary",), vmem_limit_bytes=VMEM_LIMIT),
        name="ffn",
    )(x2d, pre, wg, wu, wd, post, *cast_weights)
    return outs[0], outs[1:]


def _head_norm(z, ones_bd, gain):
    z2 = z * z
    hi = z2.astype(BF16)
    lo = (z2 - hi.astype(F32)).astype(BF16)
    ssum = _dot(hi, ones_bd) + _dot(lo, ones_bd)
    return z * lax.rsqrt(ssum * (1.0 / HEAD_DIM) + EPS) * gain


def _rope(x, cos_t, sin_t, first_half):
    up = pltpu.roll(x, LANES - ROPE_NFREQ, axis=1)
    dn = pltpu.roll(x, ROPE_NFREQ, axis=1)
    return x * cos_t + jnp.where(first_half, up, dn) * sin_t


def _qkv_kernel(x_ref, pre_ref, w_ref, qg_ref, kg_ref, onesq_ref, onesk_ref, cos_ref, sin_ref,
                m_ref, mg_ref, wm_ref, q_ref, kt_ref, v_ref, km_ref, vm_ref):
    @pl.when(pl.program_id(1) == 0)
    def _():
        mn = _rms(m_ref[...], mg_ref[...]).astype(BF16)
        kv = _dot(mn, wm_ref[...])
        km_ref[...] = kv[:, :MEM_WIDTH].astype(BF16)
        vm_ref[...] = kv[:, MEM_WIDTH:].astype(BF16)

    lane = lax.broadcasted_iota(jnp.int32, (QKV_SUB, LANES), 1)
    first_half = (lane % (2 * ROPE_NFREQ)) < ROPE_NFREQ
    for r0 in range(0, x_ref.shape[0], QKV_SUB):
        r = slice(r0, r0 + QKV_SUB)
        h = _rms(x_ref[r, :], pre_ref[...]).astype(BF16)
        z = _dot(h, w_ref[...])
        cos_t = cos_ref[r, :]
        sin_t = sin_ref[r, :]
        qn = _head_norm(z[:, :ATTN_WIDTH], onesq_ref[...], qg_ref[...])
        for j in range(N_SLABS):
            q_ref[j, r, :] = _rope(qn[:, j * LANES:(j + 1) * LANES], cos_t, sin_t,
                                   first_half).astype(BF16)
        kn = _head_norm(z[:, ATTN_WIDTH:ATTN_WIDTH + KV_WIDTH], onesk_ref[...], kg_ref[...])
        kr = _rope(kn, cos_t, sin_t, first_half)
        kt = kr.T.astype(BF16)
        kt_ref[0, :, r] = kt[:HEAD_DIM]
        kt_ref[1, :, r] = kt[HEAD_DIM:]
        v_ref[r, :] = z[:, ATTN_WIDTH + KV_WIDTH:].astype(BF16)


def _qkv(x3, pre, w_qkv, qg, kg, ones_q, ones_k, cos_t, sin_t, mem, mem_g, w_mkv):
    b, s, _ = x3.shape
    tm = QKV_TM
    mem_spec = pl.BlockSpec((None, MEM_LEN, MEM_WIDTH), lambda bi, i: (bi, 0, 0))
    return pl.pallas_call(
        _qkv_kernel,
        out_shape=(
            jax.ShapeDtypeStruct((b, N_SLABS, s, LANES), BF16),
            jax.ShapeDtypeStruct((b, N_KV_HEADS, HEAD_DIM, s), BF16),
            jax.ShapeDtypeStruct((b, s, KV_WIDTH), BF16),
            jax.ShapeDtypeStruct((b, MEM_LEN, MEM_WIDTH), BF16),
            jax.ShapeDtypeStruct((b, MEM_LEN, MEM_WIDTH), BF16),
        ),
        grid=(b, s // tm),
        in_specs=[
            pl.BlockSpec((None, tm, D_MODEL), lambda bi, i: (bi, i, 0)),
            _const_spec((1, D_MODEL)),
            _const_spec((D_MODEL, QKV_WIDTH)),
            _const_spec((1, ATTN_WIDTH)),
            _const_spec((1, KV_WIDTH)),
            _const_spec((ATTN_WIDTH, ATTN_WIDTH)),
            _const_spec((KV_WIDTH, KV_WIDTH)),
            pl.BlockSpec((tm, LANES), lambda bi, i: (i, 0)),
            pl.BlockSpec((tm, LANES), lambda bi, i: (i, 0)),
            pl.BlockSpec((None, MEM_LEN, D_MODEL), lambda bi, i: (bi, 0, 0)),
            _const_spec((1, D_MODEL)),
            _const_spec((D_MODEL, 2 * MEM_WIDTH)),
        ],
        out_specs=(
            pl.BlockSpec((None, N_SLABS, tm, LANES), lambda bi, i: (bi, 0, i, 0)),
            pl.BlockSpec((None, N_KV_HEADS, HEAD_DIM, tm), lambda bi, i: (bi, 0, 0, i)),
            pl.BlockSpec((None, tm, KV_WIDTH), lambda bi, i: (bi, i, 0)),
            mem_spec,
            mem_spec,
        ),
        compiler_params=pltpu.CompilerParams(
            dimension_semantics=("arbitrary", "arbitrary"), vmem_limit_bytes=VMEM_LIMIT),
        name="qkv",
    )(x3, pre, w_qkv, qg, kg, ones_q, ones_k, cos_t, sin_t, mem, mem_g, w_mkv)


def _attn_kernel(q_ref, kt_ref, v_ref, o_ref):
    tq = q_ref.shape[1]
    seq = v_ref.shape[0]
    lane = lax.broadcasted_iota(jnp.int32, (tq, LANES), 1)
    lo_half = lane < HEAD_DIM
    ones = jnp.ones((ATT_KC, LANES), BF16)
    grp = pl.program_id(1)
    for j in range(q_ref.shape[0]):
        q = q_ref[j]
        zero = jnp.zeros_like(q)
        halves = []
        for par in range(2):
            lhs = jnp.where(lo_half, q, zero) if par == 0 else jnp.where(lo_half, zero, q)
            m = acc = None
            for c in range(seq // ATT_KC):
                ksl = slice(c * ATT_KC, (c + 1) * ATT_KC)
                kt = kt_ref[:, ksl]
                s = _dot(lhs, jnp.concatenate([kt, kt], axis=0))
                v_ext = jnp.concatenate([v_ref[ksl, :], ones], axis=1)
                mc = jnp.max(s, axis=-1, keepdims=True)
                if c == 0:
                    m = mc
                    acc = _dot(jnp.exp2(s - m).astype(BF16), v_ext)
                else:
                    m_new = jnp.maximum(m, mc)
                    acc = jnp.exp2(m - m_new) * acc + _dot(jnp.exp2(s - m_new).astype(BF16), v_ext)
                    m = m_new
            o = acc[:, :LANES] / acc[:, LANES:]
            o_sw = pltpu.roll(o, HEAD_DIM, axis=1)
            halves.append(jnp.where(grp == par, o, o_sw))
        o_ref[j] = jnp.where(lo_half, halves[0], halves[1]).astype(BF16)


def _attn(q4, kt, v):
    b, _, s, _ = q4.shape
    tq = ATT_TQ
    per_grp = N_SLABS // N_KV_HEADS
    return pl.pallas_call(
        _attn_kernel,
        out_shape=jax.ShapeDtypeStruct((b, N_SLABS, s, LANES), BF16),
        grid=(b, N_KV_HEADS, s // tq),
        in_specs=[
            pl.BlockSpec((None, per_grp, tq, LANES), lambda bi, g, i: (bi, g, i, 0)),
            pl.BlockSpec((None, None, HEAD_DIM, s), lambda bi, g, i: (bi, g, 0, 0)),
            pl.BlockSpec((None, s, KV_WIDTH), lambda bi, g, i: (bi, 0, 0)),
        ],
        out_specs=pl.BlockSpec((None, per_grp, tq, LANES), lambda bi, g, i: (bi, g, i, 0)),
        compiler_params=pltpu.CompilerParams(
            dimension_semantics=("arbitrary", "arbitrary", "arbitrary"),
            vmem_limit_bytes=VMEM_LIMIT),
        name="attn",
    )(q4, kt, v)


def _mixer_kernel(n_cast, x_ref, a_ref, pre_ref, wra_ref, wrb_ref, wbg_ref, bbg_ref, gvn_ref,
                  ws_ref, bs_ref, km_ref, vm_ref, wpa_ref, wpg_ref, wpm_ref, wo_ref, post_ref, *refs):
    o_ref = refs[n_cast]
    for w_ref, wb_ref in zip(refs[:n_cast], refs[n_cast + 1:]):
        wb_ref[...] = w_ref[...].astype(BF16)
    nch = MIX_SUB // GMLP_CHUNK
    gd = GMLP_WIDTH // GMLP_GROUPS
    for r0 in range(0, x_ref.shape[0], MIX_SUB):
        r = slice(r0, r0 + MIX_SUB)
        x = x_ref[r, :]
        hb = _rms(x, pre_ref[...]).astype(BF16)
        z = jnp.concatenate([_dot(hb, wra_ref[...]), _dot(hb, wrb_ref[...])], axis=1)
        gu = jax.nn.gelu(z[:, :GMLP_WIDTH])
        gv = jax.nn.gelu(z[:, GMLP_WIDTH:2 * GMLP_WIDTH])
        qm = z[:, 2 * GMLP_WIDTH:].astype(BF16)

        vn = _rms(gv, gvn_ref[...]).astype(BF16)
        cols = []
        for g in range(GMLP_GROUPS):
            rhs = jnp.concatenate(
                [vn[c * GMLP_CHUNK:(c + 1) * GMLP_CHUNK, g * gd:(g + 1) * gd] for c in range(nch)],
                axis=1)
            mixed = _dot(ws_ref[g].astype(BF16), rhs) + jnp.concatenate([bs_ref[g]] * nch, axis=1)
            cols.append(jnp.concatenate(
                [mixed[:, c * gd:(c + 1) * gd] for c in range(nch)], axis=0))
        gm = (gu * jnp.concatenate(cols, axis=1)).astype(BF16)

        outs = []
        for hh in range(MEM_HEADS):
            sl = slice(hh * MEM_HEAD_DIM, (hh + 1) * MEM_HEAD_DIM)
            s = lax.dot_general(qm[:, sl], km_ref[:, sl], (((1,), (1,)), ((), ())),
                                preferred_element_type=F32) * (MEM_HEAD_DIM ** -0.5)
            m = jnp.max(s, axis=-1, keepdims=True)
            p = jnp.exp(s - m)
            l = jnp.sum(p, axis=-1, keepdims=True)
            outs.append(_dot(p.astype(BF16), vm_ref[:, sl]) / l)
        mo = jnp.concatenate(outs, axis=1).astype(BF16)

        att = jnp.concatenate([a_ref[j, r, :] for j in range(N_SLABS)], axis=1)
        gates = jax.nn.sigmoid(_dot(hb, wbg_ref[...]) + bbg_ref[...])
        merged = (gates[:, :D_MODEL] * _dot(att, wpa_ref[...])
                  + gates[:, D_MODEL:2 * D_MODEL] * _dot(gm, wpg_ref[...])
                  + gates[:, 2 * D_MODEL:] * _dot(mo, wpm_ref[...]))
        out = _dot(merged.astype(BF16), wo_ref[...])
        o_ref[r, :] = x + _rms(out, post_ref[...])


def _mixer(x3, a4, pre, w_in_b, w_bg, b_bg, gvn, ws, bs, km, vm, wpa, wpg, wpm, wo, post,
           cast_layer, cast_weights):
    b, s, _ = x3.shape
    tm = MIX_TM
    n_tiles = s // tm
    gd = GMLP_WIDTH // GMLP_GROUPS
    c_in, c_out, c_shapes = _cast_specs(cast_weights, cast_layer, b * n_tiles,
                                        lambda bi, i: bi * n_tiles + i)
    w_in_block = lambda j: pl.BlockSpec((D_MODEL, QKV_WIDTH), lambda bi, i: (0, j),
                                        pipeline_mode=pl.Buffered(1))
    outs = pl.pallas_call(
        functools.partial(_mixer_kernel, len(cast_weights)),
        out_shape=[jax.ShapeDtypeStruct((b, s, D_MODEL), F32)] + c_shapes,
        grid=(b, n_tiles),
        in_specs=[
            pl.BlockSpec((None, tm, D_MODEL), lambda bi, i: (bi, i, 0)),
            pl.BlockSpec((None, N_SLABS, tm, LANES), lambda bi, i: (bi, 0, i, 0)),
            _const_spec((1, D_MODEL)),
            w_in_block(1),
            w_in_block(2),
            _const_spec((D_MODEL, 3 * D_MODEL)),
            _const_spec((1, 3 * D_MODEL)),
            _const_spec((1, GMLP_WIDTH)),
            _const_spec((GMLP_GROUPS, GMLP_CHUNK, GMLP_CHUNK)),
            _const_spec((GMLP_GROUPS, GMLP_CHUNK, gd)),
            pl.BlockSpec((None, MEM_LEN, MEM_WIDTH), lambda bi, i: (bi, 0, 0)),
            pl.BlockSpec((None, MEM_LEN, MEM_WIDTH), lambda bi, i: (bi, 0, 0)),
            _const_spec((ATTN_WIDTH, D_MODEL)),
            _const_spec((GMLP_WIDTH, D_MODEL)),
            _const_spec((MEM_WIDTH, D_MODEL)),
            _const_spec((D_MODEL, D_MODEL)),
            _const_spec((1, D_MODEL)),
        ] + c_in,
        out_specs=[pl.BlockSpec((None, tm, D_MODEL), lambda bi, i: (bi, i, 0))] + c_out,
        compiler_params=pltpu.CompilerParams(
            dimension_semantics=("arbitrary", "arbitrary"), vmem_limit_bytes=VMEM_LIMIT),
        name="mixer",
    )(x3, a4, pre, w_in_b, w_in_b, w_bg, b_bg, gvn, ws, bs, km, vm, wpa, wpg, wpm, wo, post,
      *cast_weights)
    return outs[0], outs[1:]


def _rope_tables(seq):
    f32 = np.float32
    rows = seq // GRID_W
    row = np.repeat(np.arange(rows, dtype=f32), GRID_W)
    col = np.tile(np.arange(GRID_W, dtype=f32), rows)
    inv_freq = (f32(ROPE_THETA) ** (-np.arange(ROPE_NFREQ, dtype=f32) / f32(ROPE_NFREQ))).astype(f32)
    ang = np.stack([row[:, None] * inv_freq, col[:, None] * inv_freq], axis=1)
    cos, sin = np.cos(ang).astype(f32), np.sin(ang).astype(f32)
    cos64 = np.concatenate([cos[:, 0], cos[:, 0], cos[:, 1], cos[:, 1]], axis=1)
    sin64 = np.concatenate([-sin[:, 0], sin[:, 0], -sin[:, 1], sin[:, 1]], axis=1)
    reps = LANES // HEAD_DIM
    return jnp.asarray(np.tile(cos64, (1, reps))), jnp.asarray(np.tile(sin64, (1, reps)))


def _block_ones(width):
    idx = np.arange(width) // HEAD_DIM
    return jnp.asarray((idx[:, None] == idx[None, :]).astype(np.float32), dtype=BF16)


def kernel(x, mem, ffn1_pre, ffn1_w_gate, ffn1_w_up, ffn1_w_down, ffn1_post, mix_pre, mem_norm, w_in, w_mem_kv, q_norm, k_norm, gmlp_v_norm, gmlp_w_s, gmlp_b_s, w_branch_gate, b_branch_gate, w_proj_attn, w_proj_gmlp, w_proj_mem, w_out, mix_post, ffn2_pre, ffn2_w_gate, ffn2_w_up, ffn2_w_down, ffn2_post):
    b, s, d = x.shape
    depth = w_in.shape[0]
    cos_t, sin_t = _rope_tables(s)
    ones_q = _block_ones(ATTN_WIDTH)
    ones_k = _block_ones(KV_WIDTH)
    gd = GMLP_WIDTH // GMLP_GROUPS
    row = lambda v: v.reshape(1, -1)

    for l in range(depth):
        x, (w_in_b, w_mkv_b, w_bg_b, wpa_b, wpg_b, wpm_b, wo_b) = _ffn(
            x.reshape(b * s, d), row(ffn1_pre[l]), ffn1_w_gate, ffn1_w_up, ffn1_w_down,
            row(ffn1_post[l]), own_layer=l, cast_layer=l,
            cast_weights=(w_in, w_mem_kv, w_branch_gate, w_proj_attn, w_proj_gmlp, w_proj_mem,
                          w_out))
        x = x.reshape(b, s, d)

        qg = row(jnp.tile(q_norm[l], N_Q_HEADS)) * (HEAD_DIM ** -0.5 * np.log2(np.e))
        kg = row(jnp.tile(k_norm[l], N_KV_HEADS))
        q4, kt, v, km, vm = _qkv(x, row(mix_pre[l]), w_in_b, qg, kg, ones_q, ones_k, cos_t, sin_t,
                                 mem, row(mem_norm[l]), w_mkv_b)
        a4 = _attn(q4, kt, v)
        bs = jnp.broadcast_to(gmlp_b_s[l][:, :, None], (GMLP_GROUPS, GMLP_CHUNK, gd))
        x, (wg2_b, wu2_b, wd2_b) = _mixer(
            x, a4, row(mix_pre[l]), w_in_b, w_bg_b, row(b_branch_gate[l]), row(gmlp_v_norm[l]),
            gmlp_w_s[l], bs, km, vm, wpa_b, wpg_b, wpm_b, wo_b, row(mix_post[l]),
            cast_layer=l, cast_weights=(ffn2_w_gate, ffn2_w_up, ffn2_w_down))

        x, _ = _ffn(x.reshape(b * s, d), row(ffn2_pre[l]), wg2_b, wu2_b, wd2_b, row(ffn2_post[l]))
        x = x.reshape(b, s, d)
    return x
```

```python
import functools

import numpy as np
import jax
import jax.numpy as jnp
from jax import lax
from jax.experimental import pallas as pl
from jax.experimental.pallas import tpu as pltpu

D_MODEL = 1024
MEM_LEN = 256
GRID_W = 64
EPS = 1e-6
HEAD_DIM = 64
ATTN_WIDTH = 512
N_Q_HEADS = 8
N_KV_HEADS = 2
KV_WIDTH = 128
QKV_WIDTH = ATTN_WIDTH + 2 * KV_WIDTH
ROPE_THETA = 10000.0
ROPE_NFREQ = 16
GMLP_WIDTH = 512
GMLP_GROUPS = 4
GMLP_CHUNK = 128
MEM_HEADS = 4
MEM_HEAD_DIM = 128
MEM_WIDTH = 512
D_FF = 2816

LANES = 128
BF16_SUBLANES = 16
N_SLABS = ATTN_WIDTH // LANES
VMEM_LIMIT = 56 * 1024 * 1024

FFN_TM = 1024
FFN_SUB = 512
FFN_FC = 256
FFN_STAGE_ROWS = 64
FFN_STAGE_SLOTS = 3
QKV_TM = 2048
QKV_SUB = 512
ATT_TQ = 1024
ATT_KC = 256
MIX_TM = 1024
MIX_SUB = 512

BF16 = jnp.bfloat16
F32 = jnp.float32


def _rms(x, g):
    ms = jnp.mean(x * x, axis=-1, keepdims=True)
    return x * lax.rsqrt(ms + EPS) * g


def _row_rms_scale(x):
    return lax.rsqrt(jnp.mean(x * x, axis=-1, keepdims=True) + EPS)


def _dot(a, b):
    return jnp.dot(a, b, preferred_element_type=F32)


def _const_spec(shape):
    nd = len(shape)
    return pl.BlockSpec(shape, lambda *_: (0,) * nd, pipeline_mode=pl.Buffered(1))


def _cast_block_rows(n_rows, n_steps):
    for rows in range(BF16_SUBLANES, n_rows + 1, BF16_SUBLANES):
        if n_rows % rows == 0 and n_rows // rows <= n_steps:
            return rows
    raise ValueError((n_rows, n_steps))


def _cast_specs(weights, layer, n_grid, step_of):
    in_specs, out_specs, out_shapes = [], [], []
    for w in weights:
        _, n_rows, n_cols = w.shape
        rows = _cast_block_rows(n_rows, n_grid)
        last = n_rows // rows - 1

        def blk(*ids, last=last):
            return jnp.minimum(step_of(*ids), last)

        in_specs.append(pl.BlockSpec((None, rows, n_cols), lambda *ids, blk=blk: (layer, blk(*ids), 0)))
        out_specs.append(pl.BlockSpec((rows, n_cols), lambda *ids, blk=blk: (blk(*ids), 0)))
        out_shapes.append(jax.ShapeDtypeStruct((n_rows, n_cols), BF16))
    return in_specs, out_specs, out_shapes


def _stage_copy(src_hbm, stg_ref, sem, job, c):
    slots, rows = stg_ref.shape[0], stg_ref.shape[1]
    return pltpu.make_async_copy(src_hbm.at[pl.ds(c * rows, rows), :], stg_ref.at[c % slots],
                                 sem.at[job, c % slots])


def _stage_weights(jobs, sem):
    n_chunks = [src.shape[0] // stg.shape[1] for src, _, stg in jobs]
    for k, (src, _, stg) in enumerate(jobs):
        for c in range(min(stg.shape[0], n_chunks[k])):
            _stage_copy(src, stg, sem, k, c).start()
    for c in range(max(n_chunks)):
        for k, (src, dst, stg) in enumerate(jobs):
            if c >= n_chunks[k]:
                continue
            slots, rows = stg.shape[0], stg.shape[1]
            _stage_copy(src, stg, sem, k, c).wait()
            dst[c * rows:(c + 1) * rows, :] = stg[c % slots].astype(BF16)
            if c + slots < n_chunks[k]:
                _stage_copy(src, stg, sem, k, c + slots).start()


def _ffn_kernel(own_layer, n_cast, x_ref, pre_ref, wg_in, wu_in, wd_in, post_ref, *refs):
    o_ref = refs[n_cast]
    if own_layer is None:
        wg_ref, wu_ref, wd_ref = wg_in, wu_in, wd_in
    else:
        wg_ref, wu_ref, wd_ref, stg_a, stg_b, stg_d, sem = refs[2 * n_cast + 1:]

        @pl.when(pl.program_id(0) == 0)
        def _():
            _stage_weights([(wg_in.at[own_layer], wg_ref, stg_a),
                            (wu_in.at[own_layer], wu_ref, stg_b),
                            (wd_in.at[own_layer], wd_ref, stg_d)], sem)

    for w_ref, wb_ref in zip(refs[:n_cast], refs[n_cast + 1:2 * n_cast + 1]):
        wb_ref[...] = w_ref[...].astype(BF16)

    n_sub = x_ref.shape[0] // FFN_SUB
    rows = [slice(r * FFN_SUB, (r + 1) * FFN_SUB) for r in range(n_sub)]
    hs = [(x_ref[r, :] * pre_ref[...]).astype(BF16) for r in rows]
    rs = [_row_rms_scale(x_ref[r, :]) for r in rows]
    accs = [None] * n_sub
    for c in range(D_FF // FFN_FC):
        sl = slice(c * FFN_FC, (c + 1) * FFN_FC)
        for k in range(n_sub):
            g = rs[k] * _dot(hs[k], wg_ref[:, sl])
            u = rs[k] * _dot(hs[k], wu_ref[:, sl])
            a = (g * jax.nn.sigmoid(g) * u).astype(BF16)
            d = _dot(a, wd_ref[sl, :])
            accs[k] = d if c == 0 else accs[k] + d
    for k, r in enumerate(rows):
        o_ref[r, :] = x_ref[r, :] + 0.5 * _rms(accs[k], post_ref[...])


def _ffn(x2d, pre, wg, wu, wd, post, own_layer=None, cast_layer=0, cast_weights=()):
    t = x2d.shape[0]
    n_grid = t // FFN_TM
    c_in, c_out, c_shapes = _cast_specs(cast_weights, cast_layer, n_grid, lambda i: i)
    if own_layer is None:
        w_specs = [_const_spec((D_MODEL, D_FF)), _const_spec((D_MODEL, D_FF)),
                   _const_spec((D_FF, D_MODEL))]
        scratch = []
    else:
        w_specs = [pl.BlockSpec(memory_space=pl.ANY)] * 3
        scratch = [
            pltpu.VMEM((D_MODEL, D_FF), BF16),
            pltpu.VMEM((D_MODEL, D_FF), BF16),
            pltpu.VMEM((D_FF, D_MODEL), BF16),
            pltpu.VMEM((FFN_STAGE_SLOTS, FFN_STAGE_ROWS, D_FF), F32),
            pltpu.VMEM((FFN_STAGE_SLOTS, FFN_STAGE_ROWS, D_FF), F32),
            pltpu.VMEM((FFN_STAGE_SLOTS, FFN_STAGE_ROWS * D_FF // D_MODEL, D_MODEL), F32),
            pltpu.SemaphoreType.DMA((3, FFN_STAGE_SLOTS)),
        ]
    outs = pl.pallas_call(
        functools.partial(_ffn_kernel, own_layer, len(cast_weights)),
        out_shape=[jax.ShapeDtypeStruct((t, D_MODEL), F32)] + c_shapes,
        grid=(n_grid,),
        in_specs=[pl.BlockSpec((FFN_TM, D_MODEL), lambda i: (i, 0)), _const_spec((1, D_MODEL))]
        + w_specs + [_const_spec((1, D_MODEL))] + c_in,
        out_specs=[pl.BlockSpec((FFN_TM, D_MODEL), lambda i: (i, 0))] + c_out,
        scratch_shapes=scratch,
        compiler_params=pltpu.CompilerParams(
            dimension_semantics=("arbitrary",), vmem_limit_bytes=VMEM_LIMIT),
        name="ffn",
    )(x2d, pre, wg, wu, wd, post, *cast_weights)
    return outs[0], outs[1:]


def _head_norm(z, ones_bd, gain):
    z2 = z * z
    hi = z2.astype(BF16)
    lo = (z2 - hi.astype(F32)).astype(BF16)
    ssum = _dot(hi, ones_bd) + _dot(lo, ones_bd)
    return z * lax.rsqrt(ssum * (1.0 / HEAD_DIM) + EPS) * gain


def _rope(x, cos_t, sin_t, first_half):
    up = pltpu.roll(x, LANES - ROPE_NFREQ, axis=1)
    dn = pltpu.roll(x, ROPE_NFREQ, axis=1)
    return x * cos_t + jnp.where(first_half, up, dn) * sin_t


def _qkv_kernel(x_ref, pre_ref, w_ref, qg_ref, kg_ref, onesq_ref, onesk_ref, cos_ref, sin_ref,
                m_ref, mg_ref, wm_ref, q_ref, kt_ref, v_ref, km_ref, vm_ref):
    @pl.when(pl.program_id(1) == 0)
    def _():
        mn = _rms(m_ref[...], mg_ref[...]).astype(BF16)
        kv = _dot(mn, wm_ref[...])
        km_ref[...] = kv[:, :MEM_WIDTH].astype(BF16)
        vm_ref[...] = kv[:, MEM_WIDTH:].astype(BF16)

    lane = lax.broadcasted_iota(jnp.int32, (QKV_SUB, LANES), 1)
    first_half = (lane % (2 * ROPE_NFREQ)) < ROPE_NFREQ
    for r0 in range(0, x_ref.shape[0], QKV_SUB):
        r = slice(r0, r0 + QKV_SUB)
        h = _rms(x_ref[r, :], pre_ref[...]).astype(BF16)
        z = _dot(h, w_ref[...])
        cos_t = cos_ref[r, :]
        sin_t = sin_ref[r, :]
        qn = _head_norm(z[:, :ATTN_WIDTH], onesq_ref[...], qg_ref[...])
        for j in range(N_SLABS):
            q_ref[j, r, :] = _rope(qn[:, j * LANES:(j + 1) * LANES], cos_t, sin_t,
                                   first_half).astype(BF16)
        kn = _head_norm(z[:, ATTN_WIDTH:ATTN_WIDTH + KV_WIDTH], onesk_ref[...], kg_ref[...])
        kr = _rope(kn, cos_t, sin_t, first_half)
        kt = kr.T.astype(BF16)
        kt_ref[0, :, r] = kt[:HEAD_DIM]
        kt_ref[1, :, r] = kt[HEAD_DIM:]
        v_ref[r, :] = z[:, ATTN_WIDTH + KV_WIDTH:].astype(BF16)


def _qkv(x3, pre, w_qkv, qg, kg, ones_q, ones_k, cos_t, sin_t, mem, mem_g, w_mkv):
    b, s, _ = x3.shape
    tm = QKV_TM
    mem_spec = pl.BlockSpec((None, MEM_LEN, MEM_WIDTH), lambda bi, i: (bi, 0, 0))
    return pl.pallas_call(
        _qkv_kernel,
        out_shape=(
            jax.ShapeDtypeStruct((b, N_SLABS, s, LANES), BF16),
            jax.ShapeDtypeStruct((b, N_KV_HEADS, HEAD_DIM, s), BF16),
            jax.ShapeDtypeStruct((b, s, KV_WIDTH), BF16),
            jax.ShapeDtypeStruct((b, MEM_LEN, MEM_WIDTH), BF16),
            jax.ShapeDtypeStruct((b, MEM_LEN, MEM_WIDTH), BF16),
        ),
        grid=(b, s // tm),
        in_specs=[
            pl.BlockSpec((None, tm, D_MODEL), lambda bi, i: (bi, i, 0)),
            _const_spec((1, D_MODEL)),
            _const_spec((D_MODEL, QKV_WIDTH)),
            _const_spec((1, ATTN_WIDTH)),
            _const_spec((1, KV_WIDTH)),
            _const_spec((ATTN_WIDTH, ATTN_WIDTH)),
            _const_spec((KV_WIDTH, KV_WIDTH)),
            pl.BlockSpec((tm, LANES), lambda bi, i: (i, 0)),
            pl.BlockSpec((tm, LANES), lambda bi, i: (i, 0)),
            pl.BlockSpec((None, MEM_LEN, D_MODEL), lambda bi, i: (bi, 0, 0)),
            _const_spec((1, D_MODEL)),
            _const_spec((D_MODEL, 2 * MEM_WIDTH)),
        ],
        out_specs=(
            pl.BlockSpec((None, N_SLABS, tm, LANES), lambda bi, i: (bi, 0, i, 0)),
            pl.BlockSpec((None, N_KV_HEADS, HEAD_DIM, tm), lambda bi, i: (bi, 0, 0, i)),
            pl.BlockSpec((None, tm, KV_WIDTH), lambda bi, i: (bi, i, 0)),
            mem_spec,
            mem_spec,
        ),
        compiler_params=pltpu.CompilerParams(
            dimension_semantics=("arbitrary", "arbitrary"), vmem_limit_bytes=VMEM_LIMIT),
        name="qkv",
    )(x3, pre, w_qkv, qg, kg, ones_q, ones_k, cos_t, sin_t, mem, mem_g, w_mkv)


def _attn_kernel(q_ref, kt_ref, v_ref, o_ref):
    tq = q_ref.shape[1]
    seq = v_ref.shape[0]
    lane = lax.broadcasted_iota(jnp.int32, (tq, LANES), 1)
    lo_half = lane < HEAD_DIM
    ones = jnp.ones((ATT_KC, LANES), BF16)
    grp = pl.program_id(1)
    for j in range(q_ref.shape[0]):
        q = q_ref[j]
        zero = jnp.zeros_like(q)
        halves = []
        for par in range(2):
            lhs = jnp.where(lo_half, q, zero) if par == 0 else jnp.where(lo_half, zero, q)
            m = acc = None
            for c in range(seq // ATT_KC):
                ksl = slice(c * ATT_KC, (c + 1) * ATT_KC)
                kt = kt_ref[:, ksl]
                s = _dot(lhs, jnp.concatenate([kt, kt], axis=0))
                v_ext = jnp.concatenate([v_ref[ksl, :], ones], axis=1)
                mc = jnp.max(s, axis=-1, keepdims=True)
                if c == 0:
                    m = mc
                    acc = _dot(jnp.exp2(s - m).astype(BF16), v_ext)
                else:
                    m_new = jnp.maximum(m, mc)
                    acc = jnp.exp2(m - m_new) * acc + _dot(jnp.exp2(s - m_new).astype(BF16), v_ext)
                    m = m_new
            o = acc[:, :LANES] / acc[:, LANES:]
            o_sw = pltpu.roll(o, HEAD_DIM, axis=1)
            halves.append(jnp.where(grp == par, o, o_sw))
        o_ref[j] = jnp.where(lo_half, halves[0], halves[1]).astype(BF16)


def _attn(q4, kt, v):
    b, _, s, _ = q4.shape
    tq = ATT_TQ
    per_grp = N_SLABS // N_KV_HEADS
    return pl.pallas_call(
        _attn_kernel,
        out_shape=jax.ShapeDtypeStruct((b, N_SLABS, s, LANES), BF16),
        grid=(b, N_KV_HEADS, s // tq),
        in_specs=[
            pl.BlockSpec((None, per_grp, tq, LANES), lambda bi, g, i: (bi, g, i, 0)),
            pl.BlockSpec((None, None, HEAD_DIM, s), lambda bi, g, i: (bi, g, 0, 0)),
            pl.BlockSpec((None, s, KV_WIDTH), lambda bi, g, i: (bi, 0, 0)),
        ],
        out_specs=pl.BlockSpec((None, per_grp, tq, LANES), lambda bi, g, i: (bi, g, i, 0)),
        compiler_params=pltpu.CompilerParams(
            dimension_semantics=("arbitrary", "arbitrary", "arbitrary"),
            vmem_limit_bytes=VMEM_LIMIT),
        name="attn",
    )(q4, kt, v)


def _mixer_kernel(n_cast, x_ref, a_ref, pre_ref, wra_ref, wrb_ref, wbg_ref, bbg_ref, gvn_ref,
                  ws_ref, bs_ref, km_ref, vm_ref, wpa_ref, wpg_ref, wpm_ref, wo_ref, post_ref, *refs):
    o_ref = refs[n_cast]
    for w_ref, wb_ref in zip(refs[:n_cast], refs[n_cast + 1:]):
        wb_ref[...] = w_ref[...].astype(BF16)
    nch = MIX_SUB // GMLP_CHUNK
    gd = GMLP_WIDTH // GMLP_GROUPS
    for r0 in range(0, x_ref.shape[0], MIX_SUB):
        r = slice(r0, r0 + MIX_SUB)
        x = x_ref[r, :]
        hb = _rms(x, pre_ref[...]).astype(BF16)
        z = jnp.concatenate([_dot(hb, wra_ref[...]), _dot(hb, wrb_ref[...])], axis=1)
        gu = jax.nn.gelu(z[:, :GMLP_WIDTH])
        gv = jax.nn.gelu(z[:, GMLP_WIDTH:2 * GMLP_WIDTH])
        qm = z[:, 2 * GMLP_WIDTH:].astype(BF16)

        vn = _rms(gv, gvn_ref[...]).astype(BF16)
        cols = []
        for g in range(GMLP_GROUPS):
            rhs = jnp.concatenate(
                [vn[c * GMLP_CHUNK:(c + 1) * GMLP_CHUNK, g * gd:(g + 1) * gd] for c in range(nch)],
                axis=1)
            mixed = _dot(ws_ref[g].astype(BF16), rhs) + jnp.concatenate([bs_ref[g]] * nch, axis=1)
            cols.append(jnp.concatenate(
                [mixed[:, c * gd:(c + 1) * gd] for c in range(nch)], axis=0))
        gm = (gu * jnp.concatenate(cols, axis=1)).astype(BF16)

        outs = []
        for hh in range(MEM_HEADS):
            sl = slice(hh * MEM_HEAD_DIM, (hh + 1) * MEM_HEAD_DIM)
            s = lax.dot_general(qm[:, sl], km_ref[:, sl], (((1,), (1,)), ((), ())),
                                preferred_element_type=F32) * (MEM_HEAD_DIM ** -0.5)
            m = jnp.max(s, axis=-1, keepdims=True)
            p = jnp.exp(s - m)
            l = jnp.sum(p, axis=-1, keepdims=True)
            outs.append(_dot(p.astype(BF16), vm_ref[:, sl]) / l)
        mo = jnp.concatenate(outs, axis=1).astype(BF16)

        att = jnp.concatenate([a_ref[j, r, :] for j in range(N_SLABS)], axis=1)
        gates = jax.nn.sigmoid(_dot(hb, wbg_ref[...]) + bbg_ref[...])
        merged = (gates[:, :D_MODEL] * _dot(att, wpa_ref[...])
                  + gates[:, D_MODEL:2 * D_MODEL] * _dot(gm, wpg_ref[...])
                  + gates[:, 2 * D_MODEL:] * _dot(mo, wpm_ref[...]))
        out = _dot(merged.astype(BF16), wo_ref[...])
        o_ref[r, :] = x + _rms(out, post_ref[...])


def _mixer(x3, a4, pre, w_in_b, w_bg, b_bg, gvn, ws, bs, km, vm, wpa, wpg, wpm, wo, post,
           cast_layer, cast_weights):
    b, s, _ = x3.shape
    tm = MIX_TM
    n_tiles = s // tm
    gd = GMLP_WIDTH // GMLP_GROUPS
    c_in, c_out, c_shapes = _cast_specs(cast_weights, cast_layer, b * n_tiles,
                                        lambda bi, i: bi * n_tiles + i)
    w_in_block = lambda j: pl.BlockSpec((D_MODEL, QKV_WIDTH), lambda bi, i: (0, j),
                                        pipeline_mode=pl.Buffered(1))
    outs = pl.pallas_call(
        functools.partial(_mixer_kernel, len(cast_weights)),
        out_shape=[jax.ShapeDtypeStruct((b, s, D_MODEL), F32)] + c_shapes,
        grid=(b, n_tiles),
        in_specs=[
            pl.BlockSpec((None, tm, D_MODEL), lambda bi, i: (bi, i, 0)),
            pl.BlockSpec((None, N_SLABS, tm, LANES), lambda bi, i: (bi, 0, i, 0)),
            _const_spec((1, D_MODEL)),
            w_in_block(1),
            w_in_block(2),
            _const_spec((D_MODEL, 3 * D_MODEL)),
            _const_spec((1, 3 * D_MODEL)),
            _const_spec((1, GMLP_WIDTH)),
            _const_spec((GMLP_GROUPS, GMLP_CHUNK, GMLP_CHUNK)),
            _const_spec((GMLP_GROUPS, GMLP_CHUNK, gd)),
            pl.BlockSpec((None, MEM_LEN, MEM_WIDTH), lambda bi, i: (bi, 0, 0)),
            pl.BlockSpec((None, MEM_LEN, MEM_WIDTH), lambda bi, i: (bi, 0, 0)),
            _const_spec((ATTN_WIDTH, D_MODEL)),
            _const_spec((GMLP_WIDTH, D_MODEL)),
            _const_spec((MEM_WIDTH, D_MODEL)),
            _const_spec((D_MODEL, D_MODEL)),
            _const_spec((1, D_MODEL)),
        ] + c_in,
        out_specs=[pl.BlockSpec((None, tm, D_MODEL), lambda bi, i: (bi, i, 0))] + c_out,
        compiler_params=pltpu.CompilerParams(
            dimension_semantics=("arbitrary", "arbitrary"), vmem_limit_bytes=VMEM_LIMIT),
        name="mixer",
    )(x3, a4, pre, w_in_b, w_in_b, w_bg, b_bg, gvn, ws, bs, km, vm, wpa, wpg, wpm, wo, post,
      *cast_weights)
    return outs[0], outs[1:]


def _rope_tables(seq):
    f32 = np.float32
    rows = seq // GRID_W
    row = np.repeat(np.arange(rows, dtype=f32), GRID_W)
    col = np.tile(np.arange(GRID_W, dtype=f32), rows)
    inv_freq = (f32(ROPE_THETA) ** (-np.arange(ROPE_NFREQ, dtype=f32) / f32(ROPE_NFREQ))).astype(f32)
    ang = np.stack([row[:, None] * inv_freq, col[:, None] * inv_freq], axis=1)
    cos, sin = np.cos(ang).astype(f32), np.sin(ang).astype(f32)
    cos64 = np.concatenate([cos[:, 0], cos[:, 0], cos[:, 1], cos[:, 1]], axis=1)
    sin64 = np.concatenate([-sin[:, 0], sin[:, 0], -sin[:, 1], sin[:, 1]], axis=1)
    reps = LANES // HEAD_DIM
    return jnp.asarray(np.tile(cos64, (1, reps))), jnp.asarray(np.tile(sin64, (1, reps)))


def _block_ones(width):
    idx = np.arange(width) // HEAD_DIM
    return jnp.asarray((idx[:, None] == idx[None, :]).astype(np.float32), dtype=BF16)


def kernel(x, mem, ffn1_pre, ffn1_w_gate, ffn1_w_up, ffn1_w_down, ffn1_post, mix_pre, mem_norm, w_in, w_mem_kv, q_norm, k_norm, gmlp_v_norm, gmlp_w_s, gmlp_b_s, w_branch_gate, b_branch_gate, w_proj_attn, w_proj_gmlp, w_proj_mem, w_out, mix_post, ffn2_pre, ffn2_w_gate, ffn2_w_up, ffn2_w_down, ffn2_post):
    b, s, d = x.shape
    depth = w_in.shape[0]
    cos_t, sin_t = _rope_tables(s)
    ones_q = _block_ones(ATTN_WIDTH)
    ones_k = _block_ones(KV_WIDTH)
    gd = GMLP_WIDTH // GMLP_GROUPS
    row = lambda v: v.reshape(1, -1)

    for l in range(depth):
        x, (w_in_b, w_mkv_b, w_bg_b, wpa_b, wpg_b, wpm_b, wo_b) = _ffn(
            x.reshape(b * s, d), row(ffn1_pre[l]), ffn1_w_gate, ffn1_w_up, ffn1_w_down,
            row(ffn1_post[l]), own_layer=l, cast_layer=l,
            cast_weights=(w_in, w_mem_kv, w_branch_gate, w_proj_attn, w_proj_gmlp, w_proj_mem,
                          w_out))
        x = x.reshape(b, s, d)

        qg = row(jnp.tile(q_norm[l], N_Q_HEADS)) * (HEAD_DIM ** -0.5 * np.log2(np.e))
        kg = row(jnp.tile(k_norm[l], N_KV_HEADS))
        q4, kt, v, km, vm = _qkv(x, row(mix_pre[l]), w_in_b, qg, kg, ones_q, ones_k, cos_t, sin_t,
                                 mem, row(mem_norm[l]), w_mkv_b)
        a4 = _attn(q4, kt, v)
        bs = jnp.broadcast_to(gmlp_b_s[l][:, :, None], (GMLP_GROUPS, GMLP_CHUNK, gd))
        x, (wg2_b, wu2_b, wd2_b) = _mixer(
            x, a4, row(mix_pre[l]), w_in_b, w_bg_b, row(b_branch_gate[l]), row(gmlp_v_norm[l]),
            gmlp_w_s[l], bs, km, vm, wpa_b, wpg_b, wpm_b, wo_b, row(mix_post[l]),
            cast_layer=l, cast_weights=(ffn2_w_gate, ffn2_w_up, ffn2_w_down))

        x, _ = _ffn(x.reshape(b * s, d), row(ffn2_pre[l]), wg2_b, wu2_b, wd2_b, row(ffn2_post[l]))
        x = x.reshape(b, s, d)
    return x
```
